```python
import functools
import jax, jax.numpy as jnp
from jax import lax
import numpy as np

D_MODEL = 2048
BATCH = 2
SEQ = 4096
DEPTH = 4
DEC_BATCH = 8
DEC_SEQ = 1
PAST_LEN = 16384
PAGE_SIZE = 128

SB_HEADS = 8
SB_HEAD_DIM = 128
SB_WIDTH = SB_HEADS * SB_HEAD_DIM
SB_BLOCK = 128
SB_BIAS_INIT = -6.0
HG_HEADS = 8
HG_KEY_DIM = 128
HG_VAL_DIM = 128
HG_KEY_WIDTH = HG_HEADS * HG_KEY_DIM
HG_VAL_WIDTH = HG_HEADS * HG_VAL_DIM
HG_CHUNK = 64
D_FF = 4 * D_MODEL
RMS_EPS = 1e-6
IN_SPLITS = (SB_WIDTH, SB_WIDTH, SB_WIDTH, HG_KEY_WIDTH, HG_KEY_WIDTH, HG_VAL_WIDTH, HG_VAL_WIDTH, D_MODEL, D_MODEL)
IN_OFFSETS = [int(o) for o in np.cumsum(IN_SPLITS)[:-1]]
D_IN = int(sum(IN_SPLITS))

kernel_name = "hybrid_stickbreaking_hgrn2_decode_step"


def rmsnorm(x, w):
    xf = x.astype(jnp.float32)
    out = xf * lax.rsqrt(jnp.mean(xf * xf, axis=-1, keepdims=True) + RMS_EPS) * w.astype(jnp.float32)
    return out.astype(x.dtype)


def stick_breaking(q, k, v, bias, q_pos, k_pos):
    z = (jnp.einsum('bthd,bshd->bhts', q, k).astype(jnp.float32) * (SB_HEAD_DIM ** -0.5)
         + bias.astype(jnp.float32)[None, :, None, None])
    mask = k_pos[None, :] < q_pos[:, None]
    log_1m = jnp.where(mask, jax.nn.log_sigmoid(-z), 0.0)
    suffix = lax.cumsum(log_1m, axis=3, reverse=True) - log_1m
    weights = jnp.exp(jnp.where(mask, jax.nn.log_sigmoid(z) + suffix, -jnp.inf))
    return jnp.einsum('bhts,bshd->bthd', weights.astype(v.dtype), v)


def sb_prompt(q, k, v, bias):
    B, T, H, D = q.shape
    nblk = T // SB_BLOCK
    qb = q.reshape(B, nblk, SB_BLOCK, H, D).transpose(1, 0, 2, 3, 4)
    pos = jnp.arange(T, dtype=jnp.int32)
    out = lax.map(lambda a: stick_breaking(a[0], k, v, bias, a[1], pos), (qb, pos.reshape(nblk, SB_BLOCK)))
    return out.transpose(1, 0, 2, 3, 4).reshape(B, T, H, D)


def sb_sample(q, k_new, v_new, bias, cache_k_l, cache_v_l, page_table):
    DB, T, H, D = q.shape
    past_k = cache_k_l[page_table].reshape(DB, -1, H, D)
    past_v = cache_v_l[page_table].reshape(DB, -1, H, D)
    past = past_k.shape[1]
    k_all = jnp.concatenate([past_k, k_new.astype(past_k.dtype)], axis=1)
    v_all = jnp.concatenate([past_v, v_new.astype(past_v.dtype)], axis=1)
    q_pos = past + jnp.arange(T, dtype=jnp.int32)
    k_pos = jnp.arange(past + T, dtype=jnp.int32)
    return stick_breaking(q, k_all.astype(q.dtype), v_all.astype(q.dtype), bias, q_pos, k_pos)


def hgrn2_chunk(S, q, k, v, logf):
    C = q.shape[1]
    b = jnp.cumsum(logf, axis=1)
    o_inter = jnp.einsum('bthk,bhkv->bthv', q * jnp.exp(b), S)
    mask = jnp.tril(jnp.ones((C, C), dtype=bool))[None, :, :, None, None]
    decay = jnp.exp(jnp.where(mask, b[:, :, None] - b[:, None, :], -jnp.inf))
    scores = jnp.einsum('bthk,bshk,btshk->bhts', q, k, decay)
    o_intra = jnp.einsum('bhts,bshv->bthv', scores, v)
    b_last = b[:, -1]
    S_new = jnp.exp(b_last)[..., None] * S + jnp.einsum('bshk,bshv->bhkv', k * jnp.exp(b_last[:, None] - b), v)
    return S_new, o_inter + o_intra


def hgrn_prompt(q, k, v, logf):
    B, T, H, K = q.shape
    n = T // HG_CHUNK
    to_chunks = lambda t: t.reshape(B, n, HG_CHUNK, H, t.shape[-1]).transpose(1, 0, 2, 3, 4)
    S0 = jnp.zeros((B, H, HG_KEY_DIM, HG_VAL_DIM), jnp.float32)
    S_fin, o = lax.scan(lambda S, xs: hgrn2_chunk(S, *xs), S0,
                        (to_chunks(q), to_chunks(k), to_chunks(v), to_chunks(logf)))
    return S_fin, o.transpose(1, 0, 2, 3, 4).reshape(B, T, H, HG_VAL_DIM)


def layer(x, sb_fn, hg_fn, lb, mix_pre, mix_post, mlp_pre, mlp_post, w_in, sb_bias, hg_norm, w_ba, w_bb, w_o, w_up, w_down):
    B, T, _ = x.shape
    xn = rmsnorm(x, mix_pre)
    qa, ka, va, qb, fb, ib, gb, gate_a, gate_b = jnp.split(xn @ w_in, IN_OFFSETS, axis=-1)
    heads = lambda t: t.reshape(B, T, HG_HEADS, -1)
    qa = qa.reshape(B, T, SB_HEADS, SB_HEAD_DIM)
    ka = ka.reshape(B, T, SB_HEADS, SB_HEAD_DIM)
    va = va.reshape(B, T, SB_HEADS, SB_HEAD_DIM)
    o_a = sb_fn(qa, ka, va, sb_bias).reshape(B, T, SB_WIDTH)
    fpre = heads(fb.astype(jnp.float32))
    lbh = lb.reshape(HG_HEADS, HG_KEY_DIM)
    logf = jnp.logaddexp(jnp.log(lbh), jnp.log1p(-lbh) + jax.nn.log_sigmoid(fpre))
    kh = (1.0 - lbh) * jax.nn.sigmoid(-fpre)
    qh = heads(jax.nn.silu(qb).astype(jnp.float32)) * (HG_KEY_DIM ** -0.5)
    vh = heads(ib.astype(jnp.float32))
    s_new, o_b = hg_fn(qh, kh, vh, logf)
    o_b = rmsnorm(o_b, hg_norm.reshape(HG_HEADS, HG_VAL_DIM)) * heads(jax.nn.silu(gb)).astype(jnp.float32)
    o_b = o_b.reshape(B, T, HG_VAL_WIDTH).astype(x.dtype)
    merged = jax.nn.sigmoid(gate_a) * (o_a @ w_ba) + jax.nn.sigmoid(gate_b) * (o_b @ w_bb)
    x = x + rmsnorm(merged @ w_o, mix_post)
    hn = rmsnorm(x, mlp_pre)
    x = x + rmsnorm(jnp.square(jax.nn.relu(hn @ w_up)) @ w_down, mlp_post)
    return x, ka, va, s_new


def setup_inputs(seed: int = 0) -> dict:
    key = jax.random.key(seed)
    ks = jax.random.split(key, 20)
    f32 = jnp.float32
    n_pages = PAST_LEN // PAGE_SIZE
    n_used = DEC_BATCH * n_pages
    n_pool = n_used + max(1, n_used // 4)
    nrm = lambda k, shape, scale: jax.random.normal(k, shape, f32) * scale
    gain = lambda k: 1.0 + 0.05 * jax.random.normal(k, (DEPTH, D_MODEL), f32)
    page_table = jax.random.permutation(ks[5], n_pool)[:n_used].reshape(DEC_BATCH, n_pages).astype(jnp.int32)
    return {
        "x_prompt": nrm(ks[0], (BATCH, SEQ, D_MODEL), 1.0),
        "x_sample": nrm(ks[1], (DEC_BATCH, DEC_SEQ, D_MODEL), 1.0),
        "cache_k": nrm(ks[2], (DEPTH, n_pool, PAGE_SIZE, SB_HEADS, SB_HEAD_DIM), 1.0),
        "cache_v": nrm(ks[3], (DEPTH, n_pool, PAGE_SIZE, SB_HEADS, SB_HEAD_DIM), 1.0),
        "state_hgrn": nrm(ks[4], (DEPTH, DEC_BATCH, HG_HEADS, HG_KEY_DIM, HG_VAL_DIM), 0.5),
        "page_table": page_table,
        "norm_mix_pre": gain(ks[6]),
        "norm_mix_post": gain(ks[7]),
        "norm_mlp_pre": gain(ks[8]),
        "norm_mlp_post": gain(ks[9]),
        "w_in": nrm(ks[10], (DEPTH, D_MODEL, D_IN), D_MODEL ** -0.5),
        "sb_bias": SB_BIAS_INIT + 0.1 * jax.random.normal(ks[18], (DEPTH, SB_HEADS), f32),
        "lower_bounds": nrm(ks[11], (DEPTH, HG_KEY_WIDTH), 0.1),
        "hgrn_norm": 1.0 + 0.05 * jax.random.normal(ks[12], (DEPTH, HG_VAL_WIDTH), f32),
        "w_branch_a": nrm(ks[13], (DEPTH, SB_WIDTH, D_MODEL), SB_WIDTH ** -0.5),
        "w_branch_b": nrm(ks[14], (DEPTH, HG_VAL_WIDTH, D_MODEL), HG_VAL_WIDTH ** -0.5),
        "w_out": nrm(ks[15], (DEPTH, D_MODEL, D_MODEL), D_MODEL ** -0.5),
        "w_up": nrm(ks[16], (DEPTH, D_MODEL, D_FF), D_MODEL ** -0.5),
        "w_down": nrm(ks[17], (DEPTH, D_FF, D_MODEL), D_FF ** -0.5),
    }


def reference(x_prompt, x_sample, cache_k, cache_v, state_hgrn, page_table,
              norm_mix_pre, norm_mix_post, norm_mlp_pre, norm_mlp_post, w_in, sb_bias, lower_bounds,
              hgrn_norm, w_branch_a, w_branch_b, w_out, w_up, w_down):
    lb_all = jnp.cumsum(jax.nn.softmax(lower_bounds.astype(jnp.float32), axis=0), axis=0)
    lb_all = jnp.maximum(lb_all - lb_all[:1], 0.0)
    xp, xs = x_prompt, x_sample
    kp_l, vp_l, sp_l, ks_l, vs_l, ss_l = [], [], [], [], [], []
    for l in range(DEPTH):
        params = (lb_all[l], norm_mix_pre[l], norm_mix_post[l], norm_mlp_pre[l], norm_mlp_post[l], w_in[l],
                  sb_bias[l], hgrn_norm[l], w_branch_a[l], w_branch_b[l], w_out[l], w_up[l], w_down[l])
        xp, kp, vp, sp = layer(xp, sb_prompt, hgrn_prompt, *params)
        sb_s = functools.partial(sb_sample, cache_k_l=cache_k[l], cache_v_l=cache_v[l], page_table=page_table)
        hg_s = functools.partial(hgrn2_chunk, state_hgrn[l].astype(jnp.float32))
        xs, ksm, vsm, ssm = layer(xs, sb_s, hg_s, *params)
        kp_l.append(kp); vp_l.append(vp); sp_l.append(sp.astype(state_hgrn.dtype))
        ks_l.append(ksm); vs_l.append(vsm); ss_l.append(ssm.astype(state_hgrn.dtype))
    k_prompt, v_prompt, s_prompt = jnp.stack(kp_l), jnp.stack(vp_l), jnp.stack(sp_l)
    k_sample, v_sample, s_sample = jnp.stack(ks_l), jnp.stack(vs_l), jnp.stack(ss_l)
    return (xp, xs, k_prompt, v_prompt, s_prompt, k_sample, v_sample, s_sample)
```

```python
import functools

import numpy as np
import jax
import jax.numpy as jnp
from jax import lax
from jax.experimental import pallas as pl
from jax.experimental.pallas import tpu as pltpu

F32 = jnp.float32
BF16 = jnp.bfloat16
RMS_EPS = 1e-6
LANES = 128
SUBLANES = 8
VMEM_LIMIT = 56 * 1024 * 1024

NT_DIMS = (((1,), (1,)), ((), ()))


def _dot(a, b):
    return jnp.dot(a, b, preferred_element_type=F32)


def _dot_nt(a, b):
    return lax.dot_general(a, b, NT_DIMS, preferred_element_type=F32)


def _sigmoid(x):
    return 1.0 / (1.0 + jnp.exp(-x))


def _softplus(x):
    return jnp.maximum(x, 0.0) + jnp.log1p(jnp.exp(-jnp.abs(x)))


def _rms_scale(x, w):
    return x * lax.rsqrt(jnp.mean(x * x, axis=-1, keepdims=True) + RMS_EPS) * w


def _split_bf16(x, parts):
    out = []
    for _ in range(parts - 1):
        hi = x.astype(BF16)
        out.append(hi)
        x = x - hi.astype(F32)
    out.append(x.astype(BF16))
    return out


def _params(*sem):
    return pltpu.CompilerParams(dimension_semantics=sem, vmem_limit_bytes=VMEM_LIMIT)


def _norm_matmul_kernel(x_ref, g_ref, w_ref, o_ref, xn_ref):
    @pl.when(pl.program_id(1) == 0)
    def _():
        xn_ref[...] = _rms_scale(x_ref[...], g_ref[...]).astype(BF16)

    o_ref[...] = _dot(xn_ref[...], w_ref[...])


def norm_matmul(x, gain, w, *, tm, tn):
    m, d = x.shape
    n = w.shape[1]
    return pl.pallas_call(
        _norm_matmul_kernel,
        grid=(m // tm, n // tn),
        in_specs=[
            pl.BlockSpec((tm, d), lambda i, j: (i, 0)),
            pl.BlockSpec((1, d), lambda i, j: (0, 0)),
            pl.BlockSpec((d, tn), lambda i, j: (0, j)),
        ],
        out_specs=pl.BlockSpec((tm, tn), lambda i, j: (i, j)),
        out_shape=jax.ShapeDtypeStruct((m, n), F32),
        scratch_shapes=[pltpu.VMEM((tm, d), BF16)],
        compiler_params=_params("parallel", "arbitrary"),
        name="norm_in_proj",
    )(x, gain.reshape(1, d), w)


def _sb_prompt_kernel(bias_ref, q_ref, k_ref, v_ref, o_ref, kb_ref, vb_ref, *, tq, tk, scale):
    h = pl.program_id(1)
    i = pl.program_id(2)

    @pl.when(i == 0)
    def _():
        kb_ref[...] = k_ref[...].astype(BF16)
        vb_ref[...] = v_ref[...].astype(BF16)

    q = (q_ref[...] * scale).astype(BF16)
    bias = bias_ref[h]
    row = lax.broadcasted_iota(jnp.int32, (tk, tk), 0)
    col = lax.broadcasted_iota(jnp.int32, (tk, tk), 1)
    suffix_incl = jnp.where(row >= col, 1.0, 0.0).astype(BF16)
    qpos = i * tq + lax.broadcasted_iota(jnp.int32, (tq, tk), 0)
    lane = lax.broadcasted_iota(jnp.int32, (tq, tk), 1)
    nsteps = (i + 1) * (tq // tk)

    def body(step, carry):
        later, acc = carry
        ks = pl.multiple_of((nsteps - 1 - step) * tk, tk)
        kblk = kb_ref[pl.ds(ks, tk), :]
        vblk = vb_ref[pl.ds(ks, tk), :]
        z = _dot_nt(q, kblk) + bias
        mask = (ks + lane) < qpos
        sp = _softplus(z)
        log_1m = jnp.where(mask, -sp, 0.0)
        l_hi, l_lo = _split_bf16(log_1m, 2)
        incl = _dot(l_hi, suffix_incl) + _dot(l_lo, suffix_incl)
        logw = (z - sp) + (incl - log_1m) + later
        w = jnp.where(mask, jnp.exp(logw), 0.0)
        acc = acc + _dot(w.astype(BF16), vblk)
        return later + incl[:, 0:1], acc

    init = (jnp.zeros((tq, 1), F32), jnp.zeros((tq, LANES), F32))
    _, acc = lax.fori_loop(0, nsteps, body, init)
    o_ref[...] = acc.astype(o_ref.dtype)


def sb_prompt(proj, bias, *, batch, seq, heads, head_dim, tq, tk):
    assert head_dim == LANES
    nq = seq // tq
    kernel = functools.partial(_sb_prompt_kernel, tq=tq, tk=tk, scale=head_dim ** -0.5)
    return pl.pallas_call(
        kernel,
        grid=(batch, heads, nq),
        in_specs=[
            pl.BlockSpec(memory_space=pltpu.SMEM),
            pl.BlockSpec((tq, head_dim), lambda b, h, i: (b * nq + i, h)),
            pl.BlockSpec((seq, head_dim), lambda b, h, i: (b, heads + h)),
            pl.BlockSpec((seq, head_dim), lambda b, h, i: (b, 2 * heads + h)),
        ],
        out_specs=pl.BlockSpec((tq, head_dim), lambda b, h, i: (b * nq + i, h)),
        out_shape=jax.ShapeDtypeStruct((batch * seq, heads * head_dim), BF16),
        scratch_shapes=[pltpu.VMEM((seq, head_dim), BF16), pltpu.VMEM((seq, head_dim), BF16)],
        compiler_params=_params("parallel", "parallel", "arbitrary"),
        name="sb_prompt",
    )(bias, proj, proj, proj)


def _hgrn_gates(qb, fpre, lb, key_dim):
    e = jnp.exp(-jnp.abs(fpre))
    r = 1.0 / (1.0 + e)
    log_sig = jnp.minimum(fpre, 0.0) + jnp.log(r)
    sig_neg = jnp.where(fpre >= 0.0, e * r, r)
    a = jnp.log(lb)
    b = jnp.log1p(-lb) + log_sig
    logf = jnp.maximum(a, b) + jnp.log1p(jnp.exp(-jnp.abs(a - b)))
    kh = (1.0 - lb) * sig_neg
    qh = qb * _sigmoid(qb) * (key_dim ** -0.5)
    return qh, kh, logf


def _hgrn_out(o, norm_w, gate):
    return _rms_scale(o, norm_w) * (gate * _sigmoid(gate))


def _level_mask_table(chunk):
    t = np.arange(chunk)[:, None]
    s = np.arange(chunk)[None, :]
    table = np.full((chunk, chunk), -1, np.int32)
    m, lvl = SUBLANES, 0
    while m < chunk:
        sibling = (t // (2 * m) == s // (2 * m)) & (t % (2 * m) >= m) & (s % (2 * m) < m)
        table[sibling] = lvl
        m, lvl = 2 * m, lvl + 1
    return table


def _hgrn_prompt_kernel(q_ref, f_ref, i_ref, g_ref, lb_ref, nw_ref, lm_ref, o_ref, s_ref, state_ref, *, chunk):
    h = pl.program_id(1)
    ci = pl.program_id(2)
    kd = q_ref.shape[-1]

    @pl.when(ci == 0)
    def _():
        state_ref[...] = jnp.zeros_like(state_ref)

    lb = lb_ref[pl.ds(h, 1), :]
    qh, kh, logf = _hgrn_gates(q_ref[...], f_ref[...], lb, kd)
    vh = i_ref[...]
    vh_b = vh.astype(BF16)

    row = lax.broadcasted_iota(jnp.int32, (chunk, chunk), 0)
    col = lax.broadcasted_iota(jnp.int32, (chunk, chunk), 1)
    prefix_incl = jnp.where(col <= row, 1.0, 0.0).astype(BF16)
    b = sum(_dot(prefix_incl, part) for part in _split_bf16(logf, 3))

    state = state_ref[...]
    o = _dot((qh * jnp.exp(b)).astype(BF16), state.astype(BF16))

    rows = lax.broadcasted_iota(jnp.int32, (chunk, kd), 0)
    lm = lm_ref[...]
    scores = jnp.zeros((chunk, chunk), F32)
    m, lvl = SUBLANES, 0
    while m < chunk:
        bnd = jnp.concatenate(
            [jnp.broadcast_to(b[p + m - 1:p + m, :], (2 * m, kd)) for p in range(0, chunk, 2 * m)], axis=0)
        second = (rows & (2 * m - 1)) >= m
        d = b - bnd
        qt = jnp.where(second, qh * jnp.exp(jnp.minimum(d, 0.0)), 0.0)
        kt = jnp.where(second, 0.0, kh * jnp.exp(jnp.minimum(-d, 0.0)))
        scores = jnp.where(lm == lvl, _dot_nt(qt.astype(BF16), kt.astype(BF16)), scores)
        m, lvl = 2 * m, lvl + 1
    o = o + _dot(scores.astype(BF16), vh_b)

    groups = chunk // SUBLANES
    q3 = qh.reshape(groups, SUBLANES, kd)
    k3 = kh.reshape(groups, SUBLANES, kd)
    b3 = b.reshape(groups, SUBLANES, kd)
    v3 = vh.reshape(groups, SUBLANES, kd)
    sub = lax.broadcasted_iota(jnp.int32, (groups, SUBLANES, kd), 1)
    o3 = jnp.zeros((groups, SUBLANES, kd), F32)
    for s in range(SUBLANES):
        decay = jnp.exp(jnp.minimum(b3 - b3[:, s:s + 1, :], 0.0))
        a = jnp.sum(q3 * k3[:, s:s + 1, :] * decay, axis=-1, keepdims=True)
        o3 = o3 + jnp.where(sub >= s, a, 0.0) * v3[:, s:s + 1, :]
    o = o + o3.reshape(chunk, kd)

    b_last = b[chunk - 1:chunk, :]
    k_dec = kh * jnp.exp(b_last - b)
    decay_col = jnp.broadcast_to(jnp.exp(b_last), (kd, kd)).T
    state_ref[...] = decay_col * state + _dot(k_dec.T.astype(BF16), vh_b)

    o_ref[...] = _hgrn_out(o, nw_ref[pl.ds(h, 1), :], g_ref[...]).astype(o_ref.dtype)

    @pl.when(ci == pl.num_programs(2) - 1)
    def _():
        s_ref[0, 0] = state_ref[...]


def hgrn_prompt(proj, lb, norm_w, *, batch, seq, heads, key_dim, col0, chunk):
    assert key_dim == LANES
    nc = seq // chunk
    lm = jnp.asarray(_level_mask_table(chunk))

    def seg(k):
        return pl.BlockSpec((chunk, key_dim), lambda b, h, c: (b * nc + c, col0 + k * heads + h))

    whole = lambda shape: pl.BlockSpec(shape, lambda b, h, c: (0,) * len(shape))
    return pl.pallas_call(
        functools.partial(_hgrn_prompt_kernel, chunk=chunk),
        grid=(batch, heads, nc),
        in_specs=[seg(0), seg(1), seg(2), seg(3), whole((heads, key_dim)), whole((heads, key_dim)),
                  whole((chunk, chunk))],
        out_specs=[
            pl.BlockSpec((chunk, key_dim), lambda b, h, c: (b * nc + c, h)),
            pl.BlockSpec((1, 1, key_dim, key_dim), lambda b, h, c: (b, h, 0, 0)),
        ],
        out_shape=[
            jax.ShapeDtypeStruct((batch * seq, heads * key_dim), BF16),
            jax.ShapeDtypeStruct((batch, heads, key_dim, key_dim), F32),
        ],
        scratch_shapes=[pltpu.VMEM((key_dim, key_dim), F32)],
        compiler_params=_params("parallel", "parallel", "arbitrary"),
        name="hgrn_prompt",
    )(proj, proj, proj, proj, lb, norm_w, lm)


def _merge_out_kernel(oa_ref, ob_ref, *rest, gate_blocks):
    gate_refs = rest[:2 * gate_blocks]
    x_ref, wa_ref, wb_ref, wo_ref, g_ref, o_ref = rest[2 * gate_blocks:]
    gate = lambda refs: jnp.concatenate([_sigmoid(r[...]) for r in refs], axis=1)
    a = _dot(oa_ref[...].astype(BF16), wa_ref[...])
    b = _dot(ob_ref[...].astype(BF16), wb_ref[...])
    merged = gate(gate_refs[:gate_blocks]) * a + gate(gate_refs[gate_blocks:]) * b
    y = _dot(merged.astype(BF16), wo_ref[...])
    o_ref[...] = x_ref[...] + _rms_scale(y, g_ref[...])


def merge_out(oa, ob, proj, x, wa, wb, wo, gain, *, tm, gate_off):
    m, d = x.shape
    wa_rows = wa.shape[0]
    wb_rows = wb.shape[0]
    gw = int(np.gcd(gate_off, d))
    gate_blocks = d // gw
    const = lambda shape: pl.BlockSpec(shape, lambda i: (0, 0), pipeline_mode=pl.Buffered(1))
    gate_specs = [pl.BlockSpec((tm, gw), functools.partial(lambda i, c: (i, c), c=gate_off // gw + c))
                  for c in range(2 * gate_blocks)]
    return pl.pallas_call(
        functools.partial(_merge_out_kernel, gate_blocks=gate_blocks),
        grid=(m // tm,),
        in_specs=[
            pl.BlockSpec((tm, wa_rows), lambda i: (i, 0)),
            pl.BlockSpec((tm, wb_rows), lambda i: (i, 0)),
            *gate_specs,
            pl.BlockSpec((tm, d), lambda i: (i, 0)),
            const((wa_rows, d)), const((wb_rows, d)), const((d, d)), const((1, d)),
        ],
        out_specs=pl.BlockSpec((tm, d), lambda i: (i, 0)),
        out_shape=jax.ShapeDtypeStruct((m, d), F32),
        compiler_params=_params("parallel"),
        name="merge_out_proj",
    )(oa, ob, *([proj] * (2 * gate_blocks)), x, wa, wb, wo, gain.reshape(1, d))


def _mlp_kernel(x_ref, gpre_ref, wu_ref, wd_ref, gpost_ref, o_ref, hn_ref, acc_ref):
    j = pl.program_id(1)

    @pl.when(j == 0)
    def _():
        hn_ref[...] = _rms_scale(x_ref[...], gpre_ref[...]).astype(BF16)
        acc_ref[...] = jnp.zeros_like(acc_ref)

    u = jnp.maximum(_dot(hn_ref[...], wu_ref[...]), 0.0)
    acc_ref[...] += _dot((u * u).astype(BF16), wd_ref[...])

    @pl.when(j == pl.num_programs(1) - 1)
    def _():
        o_ref[...] = x_ref[...] + _rms_scale(acc_ref[...], gpost_ref[...])


def mlp(x, gpre, wu, wd, gpost, *, tm, tf):
    m, d = x.shape
    f = wu.shape[1]
    return pl.pallas_call(
        _mlp_kernel,
        grid=(m // tm, f // tf),
        in_specs=[
            pl.BlockSpec((tm, d), lambda i, j: (i, 0)),
            pl.BlockSpec((1, d), lambda i, j: (0, 0)),
            pl.BlockSpec((d, tf), lambda i, j: (0, j)),
            pl.BlockSpec((tf, d), lambda i, j: (j, 0)),
            pl.BlockSpec((1, d), lambda i, j: (0, 0)),
        ],
        out_specs=pl.BlockSpec((tm, d), lambda i, j: (i, 0)),
        out_shape=jax.ShapeDtypeStruct((m, d), F32),
        scratch_shapes=[pltpu.VMEM((tm, d), BF16), pltpu.VMEM((tm, d), F32)],
        compiler_params=_params("parallel", "arbitrary"),
        name="mlp",
    )(x, gpre.reshape(1, d), wu, wd, gpost.reshape(1, d))


def _sb_sample_kernel(pt_ref, q_ref, bias_ref, kn_ref, vn_ref, *rest, pages_per_step, heads, page, past, scale):
    del pt_ref
    k_refs = rest[:pages_per_step]
    v_refs = rest[pages_per_step:2 * pages_per_step]
    o_ref, later_ref, acc_ref = rest[2 * pages_per_step:]
    g = pl.program_id(1)
    hp, hd = q_ref.shape[1], q_ref.shape[2]
    width = page * heads
    nblk = width // LANES

    qs = q_ref[0] * scale
    q = qs.astype(BF16)

    @pl.when(g == 0)
    def _():
        z = jnp.sum(qs * kn_ref[0], axis=-1, keepdims=True) + bias_ref[:, 0:1]
        k_pos = past + lax.broadcasted_iota(jnp.int32, (hp, 1), 1)
        mask = k_pos < past
        sp = _softplus(z)
        later_ref[...] = jnp.broadcast_to(jnp.where(mask, -sp, 0.0), (hp, LANES))
        acc_ref[...] = jnp.where(mask, jnp.exp(z - sp), 0.0) * vn_ref[0]

    head_row = lax.broadcasted_iota(jnp.int32, (hp, width), 0)
    head_lane = lax.broadcasted_iota(jnp.int32, (hp, width), 1) % heads
    own = head_row == head_lane
    r = lax.broadcasted_iota(jnp.int32, (LANES, 2 * LANES), 0)
    c = lax.broadcasted_iota(jnp.int32, (LANES, 2 * LANES), 1)
    sum_mat = jnp.where((c >= LANES) | ((r % heads == c % heads) & (r // heads >= c // heads)), 1.0, 0.0).astype(BF16)
    bias = bias_ref[...]

    later = later_ref[...]
    acc = acc_ref[...]
    for i in reversed(range(pages_per_step)):
        kp = k_refs[i][0, 0].reshape(width, hd).astype(BF16)
        vp = v_refs[i][0, 0].reshape(width, hd).astype(BF16)
        z = _dot_nt(q, kp) + bias
        sp = _softplus(z)
        log_1m = jnp.where(own, -sp, 0.0)
        stacked = jnp.concatenate([log_1m[:, b * LANES:(b + 1) * LANES] for b in range(nblk)], axis=0)
        s_hi, s_lo = _split_bf16(stacked, 2)
        sums = _dot(s_hi, sum_mat) + _dot(s_lo, sum_mat)
        pieces = [None] * nblk
        for b in reversed(range(nblk)):
            rows = slice(b * hp, (b + 1) * hp)
            blk = slice(b * LANES, (b + 1) * LANES)
            excl = sums[rows, :LANES] - log_1m[:, blk]
            logw = (z[:, blk] - sp[:, blk]) + excl + later
            pieces[b] = jnp.where(own[:, blk], jnp.exp(logw), 0.0)
            later = later + sums[rows, LANES:]
        w = jnp.concatenate(pieces, axis=1).astype(BF16)
        acc = acc + _dot(w, vp)
    later_ref[...] = later
    acc_ref[...] = acc

    @pl.when(g == pl.num_programs(1) - 1)
    def _():
        o_ref[0] = acc


def sb_sample(q, bias_b, k_new, v_new, cache_k, cache_v, page_table, *, layer, pages_per_step):
    nb, hp, hd = q.shape
    _, _, page, heads, _ = cache_k.shape
    n_pages = page_table.shape[1]
    groups = n_pages // pages_per_step
    assert LANES % heads == 0 and hd == LANES

    def page_spec(i):
        def index(b, g, pt):
            return (layer, pt[b, (groups - 1 - g) * pages_per_step + i], 0, 0, 0)
        return pl.BlockSpec((1, 1, page, heads, hd), index)

    per_sample = pl.BlockSpec((1, hp, hd), lambda b, g, pt: (b, 0, 0))
    kernel = functools.partial(_sb_sample_kernel, pages_per_step=pages_per_step, heads=heads, page=page,
                               past=n_pages * page, scale=hd ** -0.5)
    grid_spec = pltpu.PrefetchScalarGridSpec(
        num_scalar_prefetch=1,
        grid=(nb, groups),
        in_specs=[per_sample, pl.BlockSpec((hp, page * heads), lambda b, g, pt: (0, 0)), per_sample, per_sample]
        + [page_spec(i) for i in range(pages_per_step)] * 2,
        out_specs=pl.BlockSpec((1, hp, hd), lambda b, g, pt: (b, 0, 0)),
        scratch_shapes=[pltpu.VMEM((hp, LANES), F32), pltpu.VMEM((hp, hd), F32)],
    )
    return pl.pallas_call(
        kernel,
        grid_spec=grid_spec,
        out_shape=jax.ShapeDtypeStruct((nb, hp, hd), F32),
        compiler_params=_params("parallel", "arbitrary"),
        name="sb_sample",
    )(page_table, q, bias_b, k_new, v_new, *([cache_k] * pages_per_step), *([cache_v] * pages_per_step))


def _hgrn_sample_kernel(q_ref, f_ref, i_ref, g_ref, lb_ref, nw_ref, s_ref, o_ref, so_ref, *, samples):
    h = pl.program_id(0)
    kd = q_ref.shape[-1]
    lb = lb_ref[pl.ds(h, 1), :]
    qh, kh, logf = _hgrn_gates(q_ref[...], f_ref[...], lb, kd)
    vh = i_ref[...]
    f = jnp.exp(logf)
    qk = jnp.sum(qh * kh, axis=-1, keepdims=True)
    o_ref[...] = jnp.zeros_like(o_ref)
    for b in range(samples):
        row = slice(b, b + 1)
        state = s_ref[b, 0]
        f_col = jnp.broadcast_to(f[row], (kd, kd)).T
        k_col = jnp.broadcast_to(kh[row], (kd, kd)).T
        so_ref[b, 0] = f_col * state + k_col * vh[row]
        q_dec = jnp.broadcast_to(qh[row] * f[row], (SUBLANES, kd)).astype(BF16)
        o = _dot(q_dec, state.astype(BF16))[0:1] + qk[row] * vh[row]
        o_ref[row, :] = _hgrn_out(o, nw_ref[pl.ds(h, 1), :], g_ref[row, :])


def hgrn_sample(proj, lb, norm_w, state, *, layer, heads, key_dim, col0, samples):
    rows = proj.shape[0]

    def seg(k):
        return pl.BlockSpec((rows, key_dim), lambda h: (0, col0 + k * heads + h))

    whole = pl.BlockSpec((heads, key_dim), lambda h: (0, 0))
    return pl.pallas_call(
        functools.partial(_hgrn_sample_kernel, samples=samples),
        grid=(heads,),
        in_specs=[seg(0), seg(1), seg(2), seg(3), whole, whole,
                  pl.BlockSpec((None, samples, 1, key_dim, key_dim), lambda h: (layer, 0, h, 0, 0))],
        out_specs=[
            pl.BlockSpec((rows, key_dim), lambda h: (0, h)),
            pl.BlockSpec((samples, 1, key_dim, key_dim), lambda h: (0, h, 0, 0)),
        ],
        out_shape=[
            jax.ShapeDtypeStruct((rows, heads * key_dim), F32),
            jax.ShapeDtypeStruct((samples, heads, key_dim, key_dim), F32),
        ],
        compiler_params=_params("parallel"),
        name="hgrn_sample",
    )(proj, proj, proj, proj, lb, norm_w, state)


SAMPLE_ROWS = 16
PROMPT_TILES = dict(in_tm=1024, in_tn=512, sb_tq=512, sb_tk=256, hg_chunk=256, merge_tm=256, mlp_tm=512, mlp_tf=512)


def kernel(x_prompt, x_sample, cache_k, cache_v, state_hgrn, page_table, norm_mix_pre, norm_mix_post,
           norm_mlp_pre, norm_mlp_post, w_in, sb_bias, lower_bounds, hgrn_norm, w_branch_a, w_branch_b, w_out,
           w_up, w_down):
    batch, seq, d_model = x_prompt.shape
    nb, dec_seq, _ = x_sample.shape
    depth, _, page, heads, head_dim = cache_k.shape
    assert dec_seq == 1
    sb_width = heads * head_dim
    key_width = lower_bounds.shape[1]
    key_dim = key_width // heads
    hg_col0 = 3 * sb_width // key_dim
    gate_off = 3 * sb_width + 4 * key_width
    t = PROMPT_TILES

    lb_all = jnp.cumsum(jax.nn.softmax(lower_bounds.astype(F32), axis=0), axis=0)
    lb_all = jnp.maximum(lb_all - lb_all[:1], 0.0).reshape(depth, heads, key_dim)
    norm_w = hgrn_norm.reshape(depth, heads, key_dim)
    cast = lambda w: w.astype(BF16)
    w_in_b, w_a_b, w_b_b, w_o_b, w_up_b, w_down_b = map(cast, (w_in, w_branch_a, w_branch_b, w_out, w_up, w_down))

    xp = x_prompt.reshape(batch * seq, d_model)
    xs = jnp.pad(x_sample.reshape(nb, d_model), ((0, SAMPLE_ROWS - nb), (0, 0)))
    pad_heads = lambda a: jnp.pad(a.reshape(nb, heads, head_dim), ((0, 0), (0, SAMPLE_ROWS - heads), (0, 0)))
    outs = [[] for _ in range(6)]
    for l in range(depth):
        proj = norm_matmul(xp, norm_mix_pre[l], w_in_b[l], tm=t["in_tm"], tn=t["in_tn"])
        o_a = sb_prompt(proj, sb_bias[l], batch=batch, seq=seq, heads=heads, head_dim=head_dim,
                        tq=t["sb_tq"], tk=t["sb_tk"])
        o_b, s_p = hgrn_prompt(proj, lb_all[l], norm_w[l], batch=batch, seq=seq, heads=heads, key_dim=key_dim,
                               col0=hg_col0, chunk=t["hg_chunk"])
        xp = merge_out(o_a, o_b, proj, xp, w_a_b[l], w_b_b[l], w_o_b[l], norm_mix_post[l],
                       tm=t["merge_tm"], gate_off=gate_off)
        xp = mlp(xp, norm_mlp_pre[l], w_up_b[l], w_down_b[l], norm_mlp_post[l], tm=t["mlp_tm"], tf=t["mlp_tf"])
        outs[0].append(proj[:, sb_width:2 * sb_width].reshape(batch, seq, heads, head_dim))
        outs[1].append(proj[:, 2 * sb_width:3 * sb_width].reshape(batch, seq, heads, head_dim))
        outs[2].append(s_p)

        proj_s = norm_matmul(xs, norm_mix_pre[l], w_in_b[l], tm=SAMPLE_ROWS, tn=t["in_tn"])
        k_s = proj_s[:nb, sb_width:2 * sb_width]
        v_s = proj_s[:nb, 2 * sb_width:3 * sb_width]
        bias_b = jnp.pad(jnp.broadcast_to(sb_bias[l][:, None], (heads, page * heads)),
                         ((0, SAMPLE_ROWS - heads), (0, 0)))
        o_as = sb_sample(pad_heads(proj_s[:nb, :sb_width]), bias_b, pad_heads(k_s), pad_heads(v_s), cache_k, cache_v,
                         page_table, layer=l, pages_per_step=8)
        o_as = jnp.pad(o_as[:, :heads].reshape(nb, sb_width), ((0, SAMPLE_ROWS - nb), (0, 0)))
        o_bs, s_s = hgrn_sample(proj_s, lb_all[l], norm_w[l], state_hgrn, layer=l, heads=heads, key_dim=key_dim,
                                col0=hg_col0, samples=nb)
        xs = merge_out(o_as, o_bs, proj_s, xs, w_a_b[l], w_b_b[l], w_o_b[l], norm_mix_post[l],
                       tm=SAMPLE_ROWS, gate_off=gate_off)
        xs = mlp(xs, norm_mlp_pre[l], w_up_b[l], w_down_b[l], norm_mlp_post[l], tm=SAMPLE_ROWS, tf=t["mlp_tf"])
        outs[3].append(k_s.reshape(nb, dec_seq, heads, head_dim))
        outs[4].append(v_s.reshape(nb, dec_seq, heads, head_dim))
        outs[5].append(s_s)

    k_p, v_p, s_p, k_s, v_s, s_s = (jnp.stack(o) for o in outs)
    return (xp.reshape(batch, seq, d_model), xs[:nb].reshape(nb, dec_seq, d_model), k_p, v_p, s_p, k_s, v_s, s_s)
```

```python
import functools

import numpy as np
import jax
import jax.numpy as jnp
from jax import lax
from jax.experimental import pallas as pl
from jax.experimental.pallas import tpu as pltpu

F32 = jnp.float32
BF16 = jnp.bfloat16
RMS_EPS = 1e-6
LANES = 128
SUBLANES = 8
VMEM_LIMIT = 56 * 1024 * 1024

NT_DIMS = (((1,), (1,)), ((), ()))


def _dot(a, b):
    return jnp.dot(a, b, preferred_element_type=F32)


def _dot_nt(a, b):
    return lax.dot_general(a, b, NT_DIMS, preferred_element_type=F32)


def _sigmoid(x):
    return 1.0 / (1.0 + jnp.exp(-x))


def _softplus(x):
    return jnp.maximum(x, 0.0) + jnp.log(1.0 + jnp.exp(-jnp.abs(x)))


def _rms_scale(x, w):
    return x * lax.rsqrt(jnp.mean(x * x, axis=-1, keepdims=True) + RMS_EPS) * w


def _split_bf16(x, parts):
    out = []
    for _ in range(parts - 1):
        hi = x.astype(BF16)
        out.append(hi)
        x = x - hi.astype(F32)
    out.append(x.astype(BF16))
    return out


def _params(*sem):
    return pltpu.CompilerParams(dimension_semantics=sem, vmem_limit_bytes=VMEM_LIMIT)


def _norm_matmul_kernel(x_ref, g_ref, w_ref, *rest, kv_tiles):
    if kv_tiles is None:
        o_ref, xn_ref = rest
    else:
        o_ref, k_ref, v_ref, xn_ref = rest[-4:]
    j = pl.program_id(1)

    @pl.when(j == 0)
    def _():
        xn_ref[...] = _rms_scale(x_ref[...], g_ref[...]).astype(BF16)

    res = _dot(xn_ref[...], w_ref[...])
    o_ref[...] = res
    if kv_tiles is not None:
        k0, v0, n = kv_tiles

        @pl.when((j >= k0) & (j < k0 + n))
        def _():
            k_ref[...] = res

        @pl.when((j >= v0) & (j < v0 + n))
        def _():
            v_ref[...] = res


def norm_matmul(x, gain, w, *, tm, tn, kv=None):
    m, d = x.shape
    n = w.shape[1]
    in_specs = [
        pl.BlockSpec((tm, d), lambda i, j: (i, 0)),
        pl.BlockSpec((1, d), lambda i, j: (0, 0)),
        pl.BlockSpec((d, tn), lambda i, j: (0, j)),
    ]
    out_specs = [pl.BlockSpec((tm, tn), lambda i, j: (i, j))]
    out_shape = [jax.ShapeDtypeStruct((m, n), F32)]
    args = [x, gain.reshape(1, d), w]
    aliases = {}
    kv_tiles = None
    if kv is not None:
        layer, depth, col0, width, k_buf, v_buf = kv
        nt = width // tn
        kv_tiles = (col0 // tn, (col0 + width) // tn, nt)
        for first in kv_tiles[:2]:
            index = functools.partial(lambda i, j, f: (layer, i, jnp.clip(j - f, 0, nt - 1)), f=first)
            out_specs.append(pl.BlockSpec((None, tm, tn), index))
            out_shape.append(jax.ShapeDtypeStruct((depth, m, width), F32))
        if k_buf is not None:
            in_specs += [pl.BlockSpec(memory_space=pl.ANY)] * 2
            args += [k_buf, v_buf]
            aliases = {3: 1, 4: 2}
    out = pl.pallas_call(
        functools.partial(_norm_matmul_kernel, kv_tiles=kv_tiles),
        grid=(m // tm, n // tn),
        in_specs=in_specs,
        out_specs=out_specs,
        out_shape=out_shape,
        scratch_shapes=[pltpu.VMEM((tm, d), BF16)],
        input_output_aliases=aliases,
        compiler_params=_params("parallel", "arbitrary"),
        name="norm_in_proj",
    )(*args)
    return out[0] if kv is None else out


def _sb_prompt_kernel(bias_ref, q_ref, k_ref, v_ref, o_ref, kb_ref, vb_ref, z_ref, hi_ref, lo_ref, rs_ref, w_ref,
                      acc_ref, *, tq, tk, scale):
    h = pl.program_id(1)
    i = pl.program_id(2)
    n_sub = tq // tk

    @pl.when(i == 0)
    def _():
        kb_ref[...] = k_ref[...].astype(BF16)
        vb_ref[...] = v_ref[...].astype(BF16)

    q = (q_ref[...] * scale).astype(BF16)
    bias = bias_ref[h]
    row = lax.broadcasted_iota(jnp.int32, (tk, tk), 0)
    col = lax.broadcasted_iota(jnp.int32, (tk, tk), 1)
    neg_suffix = jnp.where(row >= col, -1.0, 0.0).astype(BF16)

    def scores(ks, qb, mask=None):
        z = _dot_nt(qb, kb_ref[pl.ds(ks, tk), :]) + bias
        sp = _softplus(z)
        if mask is not None:
            sp = jnp.where(mask, sp, 0.0)
        return (z, *_split_bf16(sp, 2), jnp.sum(sp, axis=-1, keepdims=True))

    def suffix(s_hi, s_lo):
        return _dot(s_hi, neg_suffix) + _dot(s_lo, neg_suffix)

    later = jnp.zeros((tq, 1), F32)
    acc = jnp.zeros((tq, LANES), F32)
    for c in reversed(range(n_sub)):
        r0 = c * tk
        local_q = lax.broadcasted_iota(jnp.int32, (tq - r0, tk), 0)
        local_k = lax.broadcasted_iota(jnp.int32, (tq - r0, tk), 1)
        mask = local_k < local_q
        ks = pl.multiple_of(i * tq + r0, tk)
        z, s_hi, s_lo, row_sum = scores(ks, q[r0:], mask)
        w = jnp.where(mask, jnp.exp(z + suffix(s_hi, s_lo) + later[r0:]), 0.0)
        a2 = acc[r0:] + _dot(w.astype(BF16), vb_ref[pl.ds(ks, tk), :])
        l2 = later[r0:] - row_sum
        later = jnp.concatenate([later[:r0], l2], axis=0) if r0 else l2
        acc = jnp.concatenate([acc[:r0], a2], axis=0) if r0 else a2
    acc_ref[...] = acc

    def put_scores(slot, ks):
        z_ref[slot], hi_ref[slot], lo_ref[slot], rs_ref[slot] = scores(ks, q)

    def put_weights(slot, later):
        incl = suffix(hi_ref[slot], lo_ref[slot])
        w_ref[slot] = jnp.exp(z_ref[slot] + incl + later).astype(BF16)
        return later - rs_ref[slot]

    def add_values(slot, ks, acc):
        return acc + _dot(w_ref[slot], vb_ref[pl.ds(ks, tk), :])

    @pl.when(i > 0)
    def _():
        nblk = i * n_sub
        ks_of = lambda n: pl.multiple_of((nblk - 1 - n) * tk, tk)
        put_scores(0, ks_of(0))
        put_scores(1, ks_of(1))
        later1 = put_weights(0, later)

        def two_trips(p, carry):
            later, acc = carry
            n = 2 * p
            for slot in (0, 1):
                acc = add_values(slot, ks_of(n + slot), acc)
                later = put_weights(1 - slot, later)
                put_scores(slot, ks_of(n + slot + 2))
            return later, acc

        later2, acc2 = lax.fori_loop(0, (nblk - 2) // 2, two_trips, (later1, acc))
        acc2 = add_values(0, ks_of(nblk - 2), acc2)
        put_weights(1, later2)
        acc_ref[...] = add_values(1, ks_of(nblk - 1), acc2)

    o_ref[...] = acc_ref[...].astype(o_ref.dtype)


def sb_prompt(proj, k_buf, v_buf, bias, *, layer, batch, seq, heads, head_dim, tq, tk):
    assert head_dim == LANES
    nq = seq // tq
    kernel = functools.partial(_sb_prompt_kernel, tq=tq, tk=tk, scale=head_dim ** -0.5)
    kv_spec = pl.BlockSpec((None, seq, head_dim), lambda b, h, i: (layer, b, h))
    return pl.pallas_call(
        kernel,
        grid=(batch, heads, nq),
        in_specs=[
            pl.BlockSpec(memory_space=pltpu.SMEM),
            pl.BlockSpec((tq, head_dim), lambda b, h, i: (b * nq + i, h)),
            kv_spec, kv_spec,
        ],
        out_specs=pl.BlockSpec((tq, head_dim), lambda b, h, i: (b * nq + i, h)),
        out_shape=jax.ShapeDtypeStruct((batch * seq, heads * head_dim), BF16),
        scratch_shapes=[
            pltpu.VMEM((seq, head_dim), BF16), pltpu.VMEM((seq, head_dim), BF16),
            pltpu.VMEM((2, tq, tk), F32), pltpu.VMEM((2, tq, tk), BF16), pltpu.VMEM((2, tq, tk), BF16),
            pltpu.VMEM((2, tq, 1), F32), pltpu.VMEM((2, tq, tk), BF16), pltpu.VMEM((tq, head_dim), F32),
        ],
        compiler_params=_params("parallel", "parallel", "arbitrary"),
        name="sb_prompt",
    )(bias, proj, k_buf, v_buf)


def _hgrn_gates(qb, fpre, lb, key_dim):
    e = jnp.exp(-jnp.abs(fpre))
    r = 1.0 / (1.0 + e)
    log_sig = jnp.minimum(fpre, 0.0) + jnp.log(r)
    sig_neg = jnp.where(fpre >= 0.0, e * r, r)
    a = jnp.log(lb)
    b = jnp.log(1.0 - lb) + log_sig
    logf = jnp.maximum(a, b) + jnp.log(1.0 + jnp.exp(-jnp.abs(a - b)))
    kh = (1.0 - lb) * sig_neg
    qh = qb * _sigmoid(qb) * (key_dim ** -0.5)
    return qh, kh, logf


def _hgrn_out(o, norm_w, gate):
    return _rms_scale(o, norm_w) * (gate * _sigmoid(gate))


def _level_mask_table(chunk):
    t = np.arange(chunk)[:, None]
    s = np.arange(chunk)[None, :]
    table = np.full((chunk, chunk), -1, np.int32)
    m, lvl = SUBLANES, 0
    while m < chunk:
        sibling = (t // (2 * m) == s // (2 * m)) & (t % (2 * m) >= m) & (s % (2 * m) < m)
        table[sibling] = lvl
        m, lvl = 2 * m, lvl + 1
    return table


def _hgrn_prompt_kernel(q_ref, f_ref, i_ref, g_ref, lb_ref, nw_ref, lm_ref, o_ref, s_ref, state_ref, *, chunk):
    h = pl.program_id(1)
    ci = pl.program_id(2)
    kd = q_ref.shape[-1]

    @pl.when(ci == 0)
    def _():
        state_ref[...] = jnp.zeros_like(state_ref)

    lb = lb_ref[pl.ds(h, 1), :]
    qh, kh, logf = _hgrn_gates(q_ref[...], f_ref[...], lb, kd)
    vh = i_ref[...]
    vh_b = vh.astype(BF16)

    row = lax.broadcasted_iota(jnp.int32, (chunk, chunk), 0)
    col = lax.broadcasted_iota(jnp.int32, (chunk, chunk), 1)
    prefix_incl = jnp.where(col <= row, 1.0, 0.0).astype(BF16)
    b = sum(_dot(prefix_incl, part) for part in _split_bf16(logf, 3))

    state = state_ref[...]
    o = _dot((qh * jnp.exp(b)).astype(BF16), state.astype(BF16))

    rows = lax.broadcasted_iota(jnp.int32, (chunk, kd), 0)
    lm = lm_ref[...]
    scores = jnp.zeros((chunk, chunk), F32)
    m, lvl = SUBLANES, 0
    while m < chunk:
        bnd = jnp.concatenate(
            [jnp.broadcast_to(b[p + m - 1:p + m, :], (2 * m, kd)) for p in range(0, chunk, 2 * m)], axis=0)
        second = (rows & (2 * m - 1)) >= m
        d = b - bnd
        qt = jnp.where(second, qh * jnp.exp(d), 0.0)
        kt = jnp.where(second, 0.0, kh * jnp.exp(-d))
        scores = jnp.where(lm == lvl, _dot_nt(qt.astype(BF16), kt.astype(BF16)), scores)
        m, lvl = 2 * m, lvl + 1
    o = o + _dot(scores.astype(BF16), vh_b)

    groups = chunk // SUBLANES
    q3 = qh.reshape(groups, SUBLANES, kd)
    k3 = kh.reshape(groups, SUBLANES, kd)
    b3 = b.reshape(groups, SUBLANES, kd)
    v3 = vh.reshape(groups, SUBLANES, kd)
    sub = lax.broadcasted_iota(jnp.int32, (groups, SUBLANES, kd), 1)
    o3 = jnp.zeros((groups, SUBLANES, kd), F32)
    for s in range(SUBLANES):
        decay = jnp.exp(b3 - b3[:, s:s + 1, :])
        a = jnp.sum(q3 * k3[:, s:s + 1, :] * decay, axis=-1, keepdims=True)
        o3 = o3 + jnp.where(sub >= s, a, 0.0) * v3[:, s:s + 1, :]
    o = o + o3.reshape(chunk, kd)

    b_last = b[chunk - 1:chunk, :]
    k_dec = kh * jnp.exp(b_last - b)
    decay_col = jnp.broadcast_to(jnp.exp(b_last), (kd, kd)).T
    state_ref[...] = decay_col * state + _dot(k_dec.T.astype(BF16), vh_b)

    o_ref[...] = _hgrn_out(o, nw_ref[pl.ds(h, 1), :], g_ref[...]).astype(o_ref.dtype)

    @pl.when(ci == pl.num_programs(2) - 1)
    def _():
        s_ref[0, 0] = state_ref[...]


def hgrn_prompt(proj, lb, norm_w, *, batch, seq, heads, key_dim, col0, chunk):
    assert key_dim == LANES
    nc = seq // chunk
    lm = jnp.asarray(_level_mask_table(chunk))

    def seg(k):
        return pl.BlockSpec((chunk, key_dim), lambda b, h, c: (b * nc + c, col0 + k * heads + h))

    whole = lambda shape: pl.BlockSpec(shape, lambda b, h, c: (0,) * len(shape))
    return pl.pallas_call(
        functools.partial(_hgrn_prompt_kernel, chunk=chunk),
        grid=(batch, heads, nc),
        in_specs=[seg(0), seg(1), seg(2), seg(3), whole((heads, key_dim)), whole((heads, key_dim)),
                  whole((chunk, chunk))],
        out_specs=[
            pl.BlockSpec((chunk, key_dim), lambda b, h, c: (b * nc + c, h)),
            pl.BlockSpec((1, 1, key_dim, key_dim), lambda b, h, c: (b, h, 0, 0)),
        ],
        out_shape=[
            jax.ShapeDtypeStruct((batch * seq, heads * key_dim), BF16),
            jax.ShapeDtypeStruct((batch, heads, key_dim, key_dim), F32),
        ],
        scratch_shapes=[pltpu.VMEM((key_dim, key_dim), F32)],
        compiler_params=_params("parallel", "parallel", "arbitrary"),
        name="hgrn_prompt",
    )(proj, proj, proj, proj, lb, norm_w, lm)


def _merge_out_kernel(oa_ref, ob_ref, *rest, gate_blocks):
    gate_refs = rest[:2 * gate_blocks]
    x_ref, wa_ref, wb_ref, wo_ref, g_ref, o_ref = rest[2 * gate_blocks:]
    gate = lambda refs: jnp.concatenate([_sigmoid(r[...]) for r in refs], axis=1)
    a = _dot(oa_ref[...].astype(BF16), wa_ref[...])
    b = _dot(ob_ref[...].astype(BF16), wb_ref[...])
    merged = gate(gate_refs[:gate_blocks]) * a + gate(gate_refs[gate_blocks:]) * b
    y = _dot(merged.astype(BF16), wo_ref[...])
    o_ref[...] = x_ref[...] + _rms_scale(y, g_ref[...])


def merge_out(oa, ob, proj, x, wa, wb, wo, gain, *, tm, gate_off):
    m, d = x.shape
    wa_rows = wa.shape[0]
    wb_rows = wb.shape[0]
    gw = int(np.gcd(gate_off, d))
    gate_blocks = d // gw
    const = lambda shape: pl.BlockSpec(shape, lambda i: (0, 0), pipeline_mode=pl.Buffered(1))
    gate_specs = [pl.BlockSpec((tm, gw), functools.partial(lambda i, c: (i, c), c=gate_off // gw + c))
                  for c in range(2 * gate_blocks)]
    return pl.pallas_call(
        functools.partial(_merge_out_kernel, gate_blocks=gate_blocks),
        grid=(m // tm,),
        in_specs=[
            pl.BlockSpec((tm, wa_rows), lambda i: (i, 0)),
            pl.BlockSpec((tm, wb_rows), lambda i: (i, 0)),
            *gate_specs,
            pl.BlockSpec((tm, d), lambda i: (i, 0)),
            const((wa_rows, d)), const((wb_rows, d)), const((d, d)), const((1, d)),
        ],
        out_specs=pl.BlockSpec((tm, d), lambda i: (i, 0)),
        out_shape=jax.ShapeDtypeStruct((m, d), F32),
        compiler_params=_params("parallel"),
        name="merge_out_proj",
    )(oa, ob, *([proj] * (2 * gate_blocks)), x, wa, wb, wo, gain.reshape(1, d))


def _mlp_kernel(x_ref, gpre_ref, wu_ref, wd_ref, gpost_ref, o_ref, hn_ref, acc_ref):
    j = pl.program_id(1)

    @pl.when(j == 0)
    def _():
        hn_ref[...] = _rms_scale(x_ref[...], gpre_ref[...]).astype(BF16)
        acc_ref[...] = jnp.zeros_like(acc_ref)

    u = jnp.maximum(_dot(hn_ref[...], wu_ref[...]), 0.0)
    acc_ref[...] += _dot((u * u).astype(BF16), wd_ref[...])

    @pl.when(j == pl.num_programs(1) - 1)
    def _():
        o_ref[...] = x_ref[...] + _rms_scale(acc_ref[...], gpost_ref[...])


def mlp(x, gpre, wu, wd, gpost, *, tm, tf):
    m, d = x.shape
    f = wu.shape[1]
    return pl.pallas_call(
        _mlp_kernel,
        grid=(m // tm, f // tf),
        in_specs=[
            pl.BlockSpec((tm, d), lambda i, j: (i, 0)),
            pl.BlockSpec((1, d), lambda i, j: (0, 0)),
            pl.BlockSpec((d, tf), lambda i, j: (0, j)),
            pl.BlockSpec((tf, d), lambda i, j: (j, 0)),
            pl.BlockSpec((1, d), lambda i, j: (0, 0)),
        ],
        out_specs=pl.BlockSpec((tm, d), lambda i, j: (i, 0)),
        out_shape=jax.ShapeDtypeStruct((m, d), F32),
        scratch_shapes=[pltpu.VMEM((tm, d), BF16), pltpu.VMEM((tm, d), F32)],
        compiler_params=_params("parallel", "arbitrary"),
        name="mlp",
    )(x, gpre.reshape(1, d), wu, wd, gpost.reshape(1, d))


def _sb_sample_kernel(pt_ref, q_ref, bias_ref, kn_ref, vn_ref, *rest, pages_per_step, heads, page, past, scale):
    del pt_ref
    k_refs = rest[:pages_per_step]
    v_refs = rest[pages_per_step:2 * pages_per_step]
    o_ref, later_ref, acc_ref = rest[2 * pages_per_step:]
    g = pl.program_id(1)
    hp, hd = q_ref.shape[1], q_ref.shape[2]
    width = page * heads
    nblk = width // LANES

    qs = q_ref[0] * scale
    q = qs.astype(BF16)

    @pl.when(g == 0)
    def _():
        z = jnp.sum(qs * kn_ref[0], axis=-1, keepdims=True) + bias_ref[:, 0:1]
        k_pos = past + lax.broadcasted_iota(jnp.int32, (hp, 1), 1)
        mask = k_pos < past
        sp = _softplus(z)
        later_ref[...] = jnp.broadcast_to(jnp.where(mask, -sp, 0.0), (hp, LANES))
        acc_ref[...] = jnp.where(mask, jnp.exp(z - sp), 0.0) * vn_ref[0]

    head_row = lax.broadcasted_iota(jnp.int32, (hp, width), 0)
    head_lane = lax.broadcasted_iota(jnp.int32, (hp, width), 1) % heads
    own = head_row == head_lane
    r = lax.broadcasted_iota(jnp.int32, (LANES, 2 * LANES), 0)
    c = lax.broadcasted_iota(jnp.int32, (LANES, 2 * LANES), 1)
    neg_sum = jnp.where((c >= LANES) | ((r % heads == c % heads) & (r // heads >= c // heads)), -1.0, 0.0).astype(BF16)
    bias = bias_ref[...]

    order = list(reversed(range(pages_per_step)))
    zs = [_dot_nt(q, k_refs[i][0, 0].reshape(width, hd).astype(BF16)) + bias for i in order]
    sps = [jnp.where(own, _softplus(z), 0.0) for z in zs]
    stacked = jnp.concatenate(
        [sp[:, b * LANES:(b + 1) * LANES] for sp in sps for b in reversed(range(nblk))], axis=0)
    s_hi, s_lo = _split_bf16(stacked, 2)
    sums = _dot(s_hi, neg_sum) + _dot(s_lo, neg_sum)

    later = later_ref[...]
    acc = acc_ref[...]
    for n, i in enumerate(order):
        pieces = [None] * nblk
        for m, b in enumerate(reversed(range(nblk))):
            rows = slice((n * nblk + m) * hp, (n * nblk + m + 1) * hp)
            blk = slice(b * LANES, (b + 1) * LANES)
            pieces[b] = jnp.where(own[:, blk], jnp.exp(zs[n][:, blk] + sums[rows, :LANES] + later), 0.0)
            later = later + sums[rows, LANES:]
        w = jnp.concatenate(pieces, axis=1).astype(BF16)
        acc = acc + _dot(w, v_refs[i][0, 0].reshape(width, hd).astype(BF16))
    later_ref[...] = later
    acc_ref[...] = acc

    @pl.when(g == pl.num_programs(1) - 1)
    def _():
        o_ref[0] = acc


def sb_sample(q, bias_b, k_new, v_new, cache_k, cache_v, page_table, *, layer, pages_per_step):
    nb, hp, hd = q.shape
    _, _, page, heads, _ = cache_k.shape
    n_pages = page_table.shape[1]
    groups = n_pages // pages_per_step
    assert LANES % heads == 0 and hd == LANES

    def page_spec(i):
        def index(b, g, pt):
            return (layer, pt[b, (groups - 1 - g) * pages_per_step + i], 0, 0, 0)
        return pl.BlockSpec((1, 1, page, heads, hd), index)

    per_sample = pl.BlockSpec((1, hp, hd), lambda b, g, pt: (b, 0, 0))
    kernel = functools.partial(_sb_sample_kernel, pages_per_step=pages_per_step, heads=heads, page=page,
                               past=n_pages * page, scale=hd ** -0.5)
    grid_spec = pltpu.PrefetchScalarGridSpec(
        num_scalar_prefetch=1,
        grid=(nb, groups),
        in_specs=[per_sample, pl.BlockSpec((hp, page * heads), lambda b, g, pt: (0, 0)), per_sample, per_sample]
        + [page_spec(i) for i in range(pages_per_step)] * 2,
        out_specs=pl.BlockSpec((1, hp, hd), lambda b, g, pt: (b, 0, 0)),
        scratch_shapes=[pltpu.VMEM((hp, LANES), F32), pltpu.VMEM((hp, hd), F32)],
    )
    return pl.pallas_call(
        kernel,
        grid_spec=grid_spec,
        out_shape=jax.ShapeDtypeStruct((nb, hp, hd), F32),
        compiler_params=_params("parallel", "arbitrary"),
        name="sb_sample",
    )(page_table, q, bias_b, k_new, v_new, *([cache_k] * pages_per_step), *([cache_v] * pages_per_step))


def _hgrn_sample_kernel(q_ref, f_ref, i_ref, g_ref, lb_ref, nw_ref, s_ref, o_ref, so_ref, *, samples):
    h = pl.program_id(0)
    kd = q_ref.shape[-1]
    lb = lb_ref[pl.ds(h, 1), :]
    qh, kh, logf = _hgrn_gates(q_ref[...], f_ref[...], lb, kd)
    vh = i_ref[...]
    f = jnp.exp(logf)
    qk = jnp.sum(qh * kh, axis=-1, keepdims=True)
    o_ref[...] = jnp.zeros_like(o_ref)
    for b in range(samples):
        row = slice(b, b + 1)
        state = s_ref[b, 0]
        f_col = jnp.broadcast_to(f[row], (kd, kd)).T
        k_col = jnp.broadcast_to(kh[row], (kd, kd)).T
        so_ref[b, 0] = f_col * state + k_col * vh[row]
        q_dec = jnp.broadcast_to(qh[row] * f[row], (SUBLANES, kd)).astype(BF16)
        o = _dot(q_dec, state.astype(BF16))[0:1] + qk[row] * vh[row]
        o_ref[row, :] = _hgrn_out(o, nw_ref[pl.ds(h, 1), :], g_ref[row, :])


def hgrn_sample(proj, lb, norm_w, state, *, layer, heads, key_dim, col0, samples):
    rows = proj.shape[0]

    def seg(k):
        return pl.BlockSpec((rows, key_dim), lambda h: (0, col0 + k * heads + h))

    whole = pl.BlockSpec((heads, key_dim), lambda h: (0, 0))
    return pl.pallas_call(
        functools.partial(_hgrn_sample_kernel, samples=samples),
        grid=(heads,),
        in_specs=[seg(0), seg(1), seg(2), seg(3), whole, whole,
                  pl.BlockSpec((None, samples, 1, key_dim, key_dim), lambda h: (layer, 0, h, 0, 0))],
        out_specs=[
            pl.BlockSpec((rows, key_dim), lambda h: (0, h)),
            pl.BlockSpec((samples, 1, key_dim, key_dim), lambda h: (0, h, 0, 0)),
        ],
        out_shape=[
            jax.ShapeDtypeStruct((rows, heads * key_dim), F32),
            jax.ShapeDtypeStruct((samples, heads, key_dim, key_dim), F32),
        ],
        compiler_params=_params("parallel"),
        name="hgrn_sample",
    )(proj, proj, proj, proj, lb, norm_w, state)


SAMPLE_ROWS = 16
PROMPT_TILES = dict(in_tm=1024, in_tn=512, sb_tq=512, sb_tk=256, hg_chunk=256, merge_tm=256, mlp_tm=512, mlp_tf=512)


def kernel(x_prompt, x_sample, cache_k, cache_v, state_hgrn, page_table, norm_mix_pre, norm_mix_post,
           norm_mlp_pre, norm_mlp_post, w_in, sb_bias, lower_bounds, hgrn_norm, w_branch_a, w_branch_b, w_out,
           w_up, w_down):
    batch, seq, d_model = x_prompt.shape
    nb, dec_seq, _ = x_sample.shape
    depth, _, page, heads, head_dim = cache_k.shape
    assert dec_seq == 1
    sb_width = heads * head_dim
    key_width = lower_bounds.shape[1]
    key_dim = key_width // heads
    hg_col0 = 3 * sb_width // key_dim
    gate_off = 3 * sb_width + 4 * key_width
    t = PROMPT_TILES

    lb_all = jnp.cumsum(jax.nn.softmax(lower_bounds.astype(F32), axis=0), axis=0)
    lb_all = jnp.maximum(lb_all - lb_all[:1], 0.0).reshape(depth, heads, key_dim)
    norm_w = hgrn_norm.reshape(depth, heads, key_dim)
    cast = lambda w: w.astype(BF16)
    w_in_b, w_a_b, w_b_b, w_o_b, w_up_b, w_down_b = map(cast, (w_in, w_branch_a, w_branch_b, w_out, w_up, w_down))

    xp = x_prompt.reshape(batch * seq, d_model)
    xs = jnp.pad(x_sample.reshape(nb, d_model), ((0, SAMPLE_ROWS - nb), (0, 0)))
    pad_heads = lambda a: jnp.pad(a.reshape(nb, heads, head_dim), ((0, 0), (0, SAMPLE_ROWS - heads), (0, 0)))
    outs = [[] for _ in range(6)]
    k_buf = v_buf = None
    for l in range(depth):
        proj, k_buf, v_buf = norm_matmul(xp, norm_mix_pre[l], w_in_b[l], tm=t["in_tm"], tn=t["in_tn"],
                                         kv=(l, depth, sb_width, sb_width, k_buf, v_buf))
        o_a = sb_prompt(proj, k_buf, v_buf, sb_bias[l], layer=l, batch=batch, seq=seq, heads=heads,
                        head_dim=head_dim, tq=t["sb_tq"], tk=t["sb_tk"])
        o_b, s_p = hgrn_prompt(proj, lb_all[l], norm_w[l], batch=batch, seq=seq, heads=heads, key_dim=key_dim,
                               col0=hg_col0, chunk=t["hg_chunk"])
        xp = merge_out(o_a, o_b, proj, xp, w_a_b[l], w_b_b[l], w_o_b[l], norm_mix_post[l],
                       tm=t["merge_tm"], gate_off=gate_off)
        xp = mlp(xp, norm_mlp_pre[l], w_up_b[l], w_down_b[l], norm_mlp_post[l], tm=t["mlp_tm"], tf=t["mlp_tf"])
        outs[2].append(s_p)

        proj_s = norm_matmul(xs, norm_mix_pre[l], w_in_b[l], tm=SAMPLE_ROWS, tn=t["in_tn"])
        k_s = proj_s[:nb, sb_width:2 * sb_width]
        v_s = proj_s[:nb, 2 * sb_width:3 * sb_width]
        bias_b = jnp.pad(jnp.broadcast_to(sb_bias[l][:, None], (heads, page * heads)),
                         ((0, SAMPLE_ROWS - heads), (0, 0)))
        o_as = sb_sample(pad_heads(proj_s[:nb, :sb_width]), bias_b, pad_heads(k_s), pad_heads(v_s), cache_k, cache_v,
                         page_table, layer=l, pages_per_step=8)
        o_as = jnp.pad(o_as[:, :heads].reshape(nb, sb_width), ((0, SAMPLE_ROWS - nb), (0, 0)))
        o_bs, s_s = hgrn_sample(proj_s, lb_all[l], norm_w[l], state_hgrn, layer=l, heads=heads, key_dim=key_dim,
                                col0=hg_col0, samples=nb)
        xs = merge_out(o_as, o_bs, proj_s, xs, w_a_b[l], w_b_b[l], w_o_b[l], norm_mix_post[l],
                       tm=SAMPLE_ROWS, gate_off=gate_off)
        xs = mlp(xs, norm_mlp_pre[l], w_up_b[l], w_down_b[l], norm_mlp_post[l], tm=SAMPLE_ROWS, tf=t["mlp_tf"])
        outs[3].append(k_s.reshape(nb, dec_seq, heads, head_dim))
        outs[4].append(v_s.reshape(nb, dec_seq, heads, head_dim))
        outs[5].append(s_s)

    s_p, k_s, v_s, s_s = (jnp.stack(o) for o in outs[2:])
    k_p, v_p = (buf.reshape(depth, batch, seq, heads, head_dim) for buf in (k_buf, v_buf))
    return (xp.reshape(batch, seq, d_model), xs[:nb].reshape(nb, dec_seq, d_model), k_p, v_p, s_p, k_s, v_s, s_s)
```

```python
import functools

import numpy as np
import jax
import jax.numpy as jnp
from jax import lax
from jax.experimental import pallas as pl
from jax.experimental.pallas import tpu as pltpu

F32 = jnp.float32
BF16 = jnp.bfloat16
RMS_EPS = 1e-6
LANES = 128
SUBLANES = 8
VMEM_LIMIT = 56 * 1024 * 1024

NT_DIMS = (((1,), (1,)), ((), ()))


def _dot(a, b):
    return jnp.dot(a, b, preferred_element_type=F32)


def _dot_nt(a, b):
    return lax.dot_general(a, b, NT_DIMS, preferred_element_type=F32)


def _sigmoid(x):
    return 1.0 / (1.0 + jnp.exp(-x))


def _softplus(x):
    return jnp.maximum(x, 0.0) + jnp.log(1.0 + jnp.exp(-jnp.abs(x)))


def _rms_scale(x, w):
    return x * lax.rsqrt(jnp.mean(x * x, axis=-1, keepdims=True) + RMS_EPS) * w


def _split_bf16(x, parts):
    out = []
    for _ in range(parts - 1):
        hi = x.astype(BF16)
        out.append(hi)
        x = x - hi.astype(F32)
    out.append(x.astype(BF16))
    return out


def _params(*sem):
    return pltpu.CompilerParams(dimension_semantics=sem, vmem_limit_bytes=VMEM_LIMIT)


def _norm_matmul_kernel(x_ref, xs_ref, g_ref, w_ref, *rest, kv_tiles):
    o_ref, os_ref, k_ref, v_ref, xn_ref = rest[-5:]
    j = pl.program_id(1)
    tm = x_ref.shape[0]

    @pl.when(j == 0)
    def _():
        xn_ref[:tm, :] = _rms_scale(x_ref[...], g_ref[...]).astype(BF16)
        xn_ref[tm:, :] = _rms_scale(xs_ref[...], g_ref[...]).astype(BF16)

    both = _dot(xn_ref[...], w_ref[...].astype(BF16))
    res = both[:tm]
    o_ref[...] = res
    os_ref[...] = both[tm:]
    k0, v0, n = kv_tiles

    @pl.when((j >= k0) & (j < k0 + n))
    def _():
        k_ref[...] = res

    @pl.when((j >= v0) & (j < v0 + n))
    def _():
        v_ref[...] = res


def norm_matmul(x, xs, gain, w_all, *, layer, tm, tn, col0, width, k_buf, v_buf):
    m, d = x.shape
    rows_s = xs.shape[0]
    depth, _, n = w_all.shape
    in_specs = [
        pl.BlockSpec((tm, d), lambda i, j: (i, 0)),
        pl.BlockSpec((rows_s, d), lambda i, j: (0, 0)),
        pl.BlockSpec((1, d), lambda i, j: (0, 0)),
        pl.BlockSpec((None, d, tn), lambda i, j: (layer, 0, j)),
    ]
    out_specs = [pl.BlockSpec((tm, tn), lambda i, j: (i, j)), pl.BlockSpec((None, rows_s, tn), lambda i, j: (i, 0, j))]
    out_shape = [jax.ShapeDtypeStruct((m, n), F32), jax.ShapeDtypeStruct((m // tm, rows_s, n), F32)]
    args = [x, xs, gain.reshape(1, d), w_all]
    nt = width // tn
    kv_tiles = (col0 // tn, (col0 + width) // tn, nt)
    for first in kv_tiles[:2]:
        index = functools.partial(lambda i, j, f: (layer, i, jnp.clip(j - f, 0, nt - 1)), f=first)
        out_specs.append(pl.BlockSpec((None, tm, tn), index))
        out_shape.append(jax.ShapeDtypeStruct((depth, m, width), F32))
    aliases = {}
    if k_buf is not None:
        in_specs += [pl.BlockSpec(memory_space=pl.ANY)] * 2
        args += [k_buf, v_buf]
        aliases = {4: 2, 5: 3}
    proj, proj_s, k_buf, v_buf = pl.pallas_call(
        functools.partial(_norm_matmul_kernel, kv_tiles=kv_tiles),
        grid=(m // tm, n // tn),
        in_specs=in_specs,
        out_specs=out_specs,
        out_shape=out_shape,
        scratch_shapes=[pltpu.VMEM((tm + rows_s, d), BF16)],
        input_output_aliases=aliases,
        compiler_params=_params("arbitrary", "arbitrary"),
        name="norm_in_proj",
    )(*args)
    return proj, proj_s[0], k_buf, v_buf


def _sb_prompt_kernel(bias_ref, q_ref, k_ref, v_ref, o_ref, kb_ref, vb_ref, z_ref, s_ref, rs_ref, w_ref, acc_ref,
                      *, tq, tk, scale):
    h = pl.program_id(1)
    i = pl.program_id(2)
    n_sub = tq // tk
    assert n_sub == 2

    @pl.when(i == 0)
    def _():
        kb_ref[...] = k_ref[...].astype(BF16)
        vb_ref[...] = v_ref[...].astype(BF16)

    q = (q_ref[...] * scale).astype(BF16)
    bias = bias_ref[h]
    row = lax.broadcasted_iota(jnp.int32, (2 * tk, tk), 0) % tk
    col = lax.broadcasted_iota(jnp.int32, (2 * tk, tk), 1)
    neg_suffix = jnp.where(row >= col, -1.0, 0.0).astype(BF16)

    nblk = (i + 1) * n_sub
    ks_of = lambda n: pl.multiple_of((nblk - 1 - n) * tk, tk)

    def diag_mask(r0):
        local_q = lax.broadcasted_iota(jnp.int32, (tq - r0, tk), 0)
        local_k = lax.broadcasted_iota(jnp.int32, (tq - r0, tk), 1)
        return local_k < local_q

    def put_scores(slot, ks, r0=0, masked=False):
        z = _dot_nt(q[r0:], kb_ref[pl.ds(ks, tk), :]) + bias
        sp = _softplus(z)
        if masked:
            sp = jnp.where(diag_mask(r0), sp, 0.0)
        z_ref[slot, r0:, :] = z
        s_ref[slot, r0:, :] = jnp.concatenate(_split_bf16(sp, 2), axis=1)
        rs_ref[slot, r0:, :] = jnp.sum(sp, axis=-1, keepdims=True)

    def put_weights(slot, later, r0=0, masked=False):
        w = jnp.exp(z_ref[slot, r0:, :] + _dot(s_ref[slot, r0:, :], neg_suffix) + later[r0:])
        if masked:
            w = jnp.where(diag_mask(r0), w, 0.0)
        w_ref[slot, r0:, :] = w.astype(BF16)
        new = later[r0:] - rs_ref[slot, r0:, :]
        return jnp.concatenate([later[:r0], new], axis=0) if r0 else new

    def add_values(slot, ks, acc, r0=0):
        new = acc[r0:] + _dot(w_ref[slot, r0:, :], vb_ref[pl.ds(ks, tk), :])
        return jnp.concatenate([acc[:r0], new], axis=0) if r0 else new

    later = jnp.zeros((tq, 1), F32)
    acc = jnp.zeros((tq, LANES), F32)
    put_scores(0, ks_of(0), tk, True)
    put_scores(1, ks_of(1), 0, True)
    later = put_weights(0, later, tk, True)

    @pl.when(i == 0)
    def _():
        a = add_values(0, ks_of(0), acc, tk)
        put_weights(1, later, 0, True)
        acc_ref[...] = add_values(1, ks_of(1), a)

    @pl.when(i > 0)
    def _():
        def trip(n, slot, later, acc, first=False):
            acc = add_values(slot, ks_of(n), acc, tk if first else 0)
            later = put_weights(1 - slot, later, 0, first)
            put_scores(slot, ks_of(n + 2))
            return later, acc

        def two_trips(p, carry):
            n = 2 * p + 2
            return trip(n + 1, 1, *trip(n, 0, *carry))

        carry = trip(1, 1, *trip(0, 0, later, acc, True))
        later2, acc2 = lax.fori_loop(0, i - 1, two_trips, carry)
        acc2 = add_values(0, ks_of(nblk - 2), acc2)
        put_weights(1, later2)
        acc_ref[...] = add_values(1, ks_of(nblk - 1), acc2)

    o_ref[...] = acc_ref[...].astype(o_ref.dtype)


def sb_prompt(proj, k_buf, v_buf, bias, *, layer, batch, seq, heads, head_dim, tq, tk):
    assert head_dim == LANES
    nq = seq // tq
    kernel = functools.partial(_sb_prompt_kernel, tq=tq, tk=tk, scale=head_dim ** -0.5)
    kv_spec = pl.BlockSpec((None, seq, head_dim), lambda b, h, i: (layer, b, h))
    return pl.pallas_call(
        kernel,
        grid=(batch, heads, nq),
        in_specs=[
            pl.BlockSpec(memory_space=pltpu.SMEM),
            pl.BlockSpec((tq, head_dim), lambda b, h, i: (b * nq + i, h)),
            kv_spec, kv_spec,
        ],
        out_specs=pl.BlockSpec((tq, head_dim), lambda b, h, i: (b * nq + i, h)),
        out_shape=jax.ShapeDtypeStruct((batch * seq, heads * head_dim), BF16),
        scratch_shapes=[
            pltpu.VMEM((seq, head_dim), BF16), pltpu.VMEM((seq, head_dim), BF16),
            pltpu.VMEM((2, tq, tk), F32), pltpu.VMEM((2, tq, 2 * tk), BF16), pltpu.VMEM((2, tq, 1), F32),
            pltpu.VMEM((2, tq, tk), BF16), pltpu.VMEM((tq, head_dim), F32),
        ],
        compiler_params=_params("parallel", "parallel", "arbitrary"),
        name="sb_prompt",
    )(bias, proj, k_buf, v_buf)


def _hgrn_gates(qb, fpre, lb, key_dim):
    e = jnp.exp(-jnp.abs(fpre))
    r = 1.0 / (1.0 + e)
    log_sig = jnp.minimum(fpre, 0.0) + jnp.log(r)
    sig_neg = jnp.where(fpre >= 0.0, e * r, r)
    a = jnp.log(lb)
    b = jnp.log(1.0 - lb) + log_sig
    logf = jnp.maximum(a, b) + jnp.log(1.0 + jnp.exp(-jnp.abs(a - b)))
    kh = (1.0 - lb) * sig_neg
    qh = qb * _sigmoid(qb) * (key_dim ** -0.5)
    return qh, kh, logf


def _hgrn_out(o, norm_w, gate):
    return _rms_scale(o, norm_w) * (gate * _sigmoid(gate))


def _level_mask_table(chunk):
    t = np.arange(chunk)[:, None]
    s = np.arange(chunk)[None, :]
    table = np.full((chunk, chunk), -1, np.int32)
    m, lvl = SUBLANES, 0
    while m < chunk:
        sibling = (t // (2 * m) == s // (2 * m)) & (t % (2 * m) >= m) & (s % (2 * m) < m)
        table[sibling] = lvl
        m, lvl = 2 * m, lvl + 1
    return table


def _hgrn_prompt_kernel(q_ref, f_ref, i_ref, g_ref, lb_ref, nw_ref, lm_ref, o_ref, s_ref, state_ref, *, chunk, hps):
    hb = pl.program_id(1)
    ci = pl.program_id(2)
    kd = q_ref.shape[-1] // hps

    @pl.when(ci == 0)
    def _():
        state_ref[...] = jnp.zeros_like(state_ref)

    row = lax.broadcasted_iota(jnp.int32, (chunk, chunk), 0)
    col = lax.broadcasted_iota(jnp.int32, (chunk, chunk), 1)
    prefix_incl = jnp.where(col <= row, 1.0, 0.0).astype(BF16)
    rows = lax.broadcasted_iota(jnp.int32, (chunk, kd), 0)
    lm = lm_ref[...]
    groups = chunk // SUBLANES
    sub = lax.broadcasted_iota(jnp.int32, (groups, SUBLANES, kd), 1)

    for hh in range(hps):
        lanes = slice(hh * kd, (hh + 1) * kd)
        lb = lb_ref[pl.ds(hb * hps + hh, 1), :]
        qh, kh, logf = _hgrn_gates(q_ref[:, lanes], f_ref[:, lanes], lb, kd)
        vh = i_ref[:, lanes]
        vh_b = vh.astype(BF16)

        b = sum(_dot(prefix_incl, part) for part in _split_bf16(logf, 3))

        state = state_ref[hh]
        o = _dot((qh * jnp.exp(b)).astype(BF16), state.astype(BF16))

        scores = jnp.zeros((chunk, chunk), F32)
        m, lvl = SUBLANES, 0
        while m < chunk:
            bnd = jnp.concatenate(
                [jnp.broadcast_to(b[p + m - 1:p + m, :], (2 * m, kd)) for p in range(0, chunk, 2 * m)], axis=0)
            second = (rows & (2 * m - 1)) >= m
            d = b - bnd
            qt = jnp.where(second, qh * jnp.exp(d), 0.0)
            kt = jnp.where(second, 0.0, kh * jnp.exp(-d))
            scores = jnp.where(lm == lvl, _dot_nt(qt.astype(BF16), kt.astype(BF16)), scores)
            m, lvl = 2 * m, lvl + 1
        o = o + _dot(scores.astype(BF16), vh_b)

        q3 = qh.reshape(groups, SUBLANES, kd)
        k3 = kh.reshape(groups, SUBLANES, kd)
        b3 = b.reshape(groups, SUBLANES, kd)
        v3 = vh.reshape(groups, SUBLANES, kd)
        o3 = jnp.zeros((groups, SUBLANES, kd), F32)
        for s in range(SUBLANES):
            decay = jnp.exp(b3 - b3[:, s:s + 1, :])
            a = jnp.sum(q3 * k3[:, s:s + 1, :] * decay, axis=-1, keepdims=True)
            o3 = o3 + jnp.where(sub >= s, a, 0.0) * v3[:, s:s + 1, :]
        o = o + o3.reshape(chunk, kd)

        b_last = b[chunk - 1:chunk, :]
        k_dec = kh * jnp.exp(b_last - b)
        decay_col = jnp.broadcast_to(jnp.exp(b_last), (kd, kd)).T
        state_ref[hh] = decay_col * state + _dot(k_dec.T.astype(BF16), vh_b)

        o_ref[:, lanes] = _hgrn_out(o, nw_ref[pl.ds(hb * hps + hh, 1), :], g_ref[:, lanes]).astype(o_ref.dtype)

    @pl.when(ci == pl.num_programs(2) - 1)
    def _():
        s_ref[0] = state_ref[...]


def hgrn_prompt(proj, lb, norm_w, *, batch, seq, heads, key_dim, col0, chunk, hps):
    assert key_dim == LANES and heads % hps == 0 and col0 % hps == 0
    nc = seq // chunk
    lm = jnp.asarray(_level_mask_table(chunk))

    def seg(k):
        return pl.BlockSpec((chunk, hps * key_dim), lambda b, h, c: (b * nc + c, (col0 + k * heads) // hps + h))

    whole = lambda shape: pl.BlockSpec(shape, lambda b, h, c: (0,) * len(shape))
    return pl.pallas_call(
        functools.partial(_hgrn_prompt_kernel, chunk=chunk, hps=hps),
        grid=(batch, heads // hps, nc),
        in_specs=[seg(0), seg(1), seg(2), seg(3), whole((heads, key_dim)), whole((heads, key_dim)),
                  whole((chunk, chunk))],
        out_specs=[
            pl.BlockSpec((chunk, hps * key_dim), lambda b, h, c: (b * nc + c, h)),
            pl.BlockSpec((1, hps, key_dim, key_dim), lambda b, h, c: (b, h, 0, 0)),
        ],
        out_shape=[
            jax.ShapeDtypeStruct((batch * seq, heads * key_dim), BF16),
            jax.ShapeDtypeStruct((batch, heads, key_dim, key_dim), F32),
        ],
        scratch_shapes=[pltpu.VMEM((hps, key_dim, key_dim), F32)],
        compiler_params=_params("parallel", "parallel", "arbitrary"),
        name="hgrn_prompt",
    )(proj, proj, proj, proj, lb, norm_w, lm)


def _merge_out_kernel(*refs, gate_blocks):
    n_in = 3 + 2 * gate_blocks
    prompt, sample = refs[:n_in], refs[n_in:2 * n_in]
    wa_ref, wb_ref, wo_ref, g_ref, o_ref, os_ref = refs[2 * n_in:]

    rows = lambda k, dtype=F32: jnp.concatenate([prompt[k][...].astype(dtype), sample[k][...].astype(dtype)], axis=0)
    gate = lambda ks: jnp.concatenate([_sigmoid(rows(k)) for k in ks], axis=1)
    a = _dot(rows(0, BF16), wa_ref[...])
    b = _dot(rows(1, BF16), wb_ref[...])
    merged = gate(range(2, 2 + gate_blocks)) * a + gate(range(2 + gate_blocks, 2 + 2 * gate_blocks)) * b
    y = _dot(merged.astype(BF16), wo_ref[...])
    out = rows(n_in - 1) + _rms_scale(y, g_ref[...])
    tm = o_ref.shape[0]
    o_ref[...] = out[:tm]
    os_ref[...] = out[tm:]


def merge_out(prompt, sample, wa_all, wb_all, wo_all, gain, *, layer, tm, gate_off):
    m, d = prompt[3].shape
    rows_s = sample[3].shape[0]
    wa_rows, wb_rows = wa_all.shape[1], wb_all.shape[1]
    gw = int(np.gcd(gate_off, d))
    gate_blocks = d // gw
    const = lambda *shape: pl.BlockSpec((None, *shape), lambda i: (layer, 0, 0), pipeline_mode=pl.Buffered(1))

    def group(rows, row_index):
        at = lambda c: (lambda i: (row_index(i), c))
        return ([pl.BlockSpec((rows, wa_rows), at(0)), pl.BlockSpec((rows, wb_rows), at(0))]
                + [pl.BlockSpec((rows, gw), at(gate_off // gw + c)) for c in range(2 * gate_blocks)]
                + [pl.BlockSpec((rows, d), at(0))])

    operands = lambda t: (t[0], t[1], *([t[2]] * (2 * gate_blocks)), t[3])
    return pl.pallas_call(
        functools.partial(_merge_out_kernel, gate_blocks=gate_blocks),
        grid=(m // tm,),
        in_specs=group(tm, lambda i: i) + group(rows_s, lambda i: 0)
        + [const(wa_rows, d), const(wb_rows, d), const(d, d), pl.BlockSpec((1, d), lambda i: (0, 0))],
        out_specs=[pl.BlockSpec((tm, d), lambda i: (i, 0)), pl.BlockSpec((rows_s, d), lambda i: (0, 0))],
        out_shape=[jax.ShapeDtypeStruct((m, d), F32), jax.ShapeDtypeStruct((rows_s, d), F32)],
        compiler_params=_params("arbitrary"),
        name="merge_out_proj",
    )(*operands(prompt), *operands(sample), wa_all, wb_all, wo_all, gain.reshape(1, d))


def _mlp_kernel(x_ref, xs_ref, gpre_ref, wu_ref, wd_ref, gpost_ref, o_ref, os_ref, hn_ref, acc_ref):
    j = pl.program_id(1)
    tm = x_ref.shape[0]

    @pl.when(j == 0)
    def _():
        hn_ref[:tm, :] = _rms_scale(x_ref[...], gpre_ref[...]).astype(BF16)
        hn_ref[tm:, :] = _rms_scale(xs_ref[...], gpre_ref[...]).astype(BF16)
        acc_ref[...] = jnp.zeros_like(acc_ref)

    u = jnp.maximum(_dot(hn_ref[...], wu_ref[...]), 0.0)
    acc_ref[...] += _dot((u * u).astype(BF16), wd_ref[...])

    @pl.when(j == pl.num_programs(1) - 1)
    def _():
        o_ref[...] = x_ref[...] + _rms_scale(acc_ref[:tm, :], gpost_ref[...])
        os_ref[...] = xs_ref[...] + _rms_scale(acc_ref[tm:, :], gpost_ref[...])


def mlp(x, xs, gpre, wu_all, wd_all, gpost, *, layer, tm, tf):
    m, d = x.shape
    rows_s = xs.shape[0]
    f = wu_all.shape[2]
    vec = pl.BlockSpec((1, d), lambda i, j: (0, 0))
    return pl.pallas_call(
        _mlp_kernel,
        grid=(m // tm, f // tf),
        in_specs=[
            pl.BlockSpec((tm, d), lambda i, j: (i, 0)),
            pl.BlockSpec((rows_s, d), lambda i, j: (0, 0)),
            vec,
            pl.BlockSpec((None, d, tf), lambda i, j: (layer, 0, j)),
            pl.BlockSpec((None, tf, d), lambda i, j: (layer, j, 0)),
            vec,
        ],
        out_specs=[pl.BlockSpec((tm, d), lambda i, j: (i, 0)), pl.BlockSpec((rows_s, d), lambda i, j: (0, 0))],
        out_shape=[jax.ShapeDtypeStruct((m, d), F32), jax.ShapeDtypeStruct((rows_s, d), F32)],
        scratch_shapes=[pltpu.VMEM((tm + rows_s, d), BF16), pltpu.VMEM((tm + rows_s, d), F32)],
        compiler_params=_params("arbitrary", "arbitrary"),
        name="mlp",
    )(x, xs, gpre.reshape(1, d), wu_all, wd_all, gpost.reshape(1, d))


def _sb_sample_kernel(pt_ref, q_ref, bias_ref, kn_ref, vn_ref, *rest, pages_per_step, heads, page, past, scale):
    del pt_ref
    k_refs = rest[:pages_per_step]
    v_refs = rest[pages_per_step:2 * pages_per_step]
    o_ref, later_ref, acc_ref = rest[2 * pages_per_step:]
    g = pl.program_id(1)
    hp, hd = q_ref.shape[1], q_ref.shape[2]
    width = page * heads
    nblk = width // LANES

    qs = q_ref[0] * scale
    q = qs.astype(BF16)

    @pl.when(g == 0)
    def _():
        z = jnp.sum(qs * kn_ref[0], axis=-1, keepdims=True) + bias_ref[:, 0:1]
        k_pos = past + lax.broadcasted_iota(jnp.int32, (hp, 1), 1)
        mask = k_pos < past
        sp = _softplus(z)
        later_ref[...] = jnp.broadcast_to(jnp.where(mask, -sp, 0.0), (hp, LANES))
        acc_ref[...] = jnp.where(mask, jnp.exp(z - sp), 0.0) * vn_ref[0]

    head_row = lax.broadcasted_iota(jnp.int32, (hp, width), 0)
    head_lane = lax.broadcasted_iota(jnp.int32, (hp, width), 1) % heads
    own = head_row == head_lane
    r = lax.broadcasted_iota(jnp.int32, (LANES, 2 * LANES), 0)
    c = lax.broadcasted_iota(jnp.int32, (LANES, 2 * LANES), 1)
    neg_sum = jnp.where((c >= LANES) | ((r % heads == c % heads) & (r // heads >= c // heads)), -1.0, 0.0).astype(BF16)
    bias = bias_ref[...]

    order = list(reversed(range(pages_per_step)))
    zs = [_dot_nt(q, k_refs[i][0, 0].reshape(width, hd).astype(BF16)) + bias for i in order]
    sps = [jnp.where(own, _softplus(z), 0.0) for z in zs]
    stacked = jnp.concatenate(
        [sp[:, b * LANES:(b + 1) * LANES] for sp in sps for b in reversed(range(nblk))], axis=0)
    s_hi, s_lo = _split_bf16(stacked, 2)
    sums = _dot(s_hi, neg_sum) + _dot(s_lo, neg_sum)

    later = later_ref[...]
    acc = acc_ref[...]
    for n, i in enumerate(order):
        pieces = [None] * nblk
        for m, b in enumerate(reversed(range(nblk))):
            rows = slice((n * nblk + m) * hp, (n * nblk + m + 1) * hp)
            blk = slice(b * LANES, (b + 1) * LANES)
            pieces[b] = jnp.where(own[:, blk], jnp.exp(zs[n][:, blk] + sums[rows, :LANES] + later), 0.0)
            later = later + sums[rows, LANES:]
        w = jnp.concatenate(pieces, axis=1).astype(BF16)
        acc = acc + _dot(w, v_refs[i][0, 0].reshape(width, hd).astype(BF16))
    later_ref[...] = later
    acc_ref[...] = acc

    @pl.when(g == pl.num_programs(1) - 1)
    def _():
        o_ref[0] = acc


def sb_sample(q, bias_b, k_new, v_new, cache_k, cache_v, page_table, *, layer, pages_per_step):
    nb, hp, hd = q.shape
    _, _, page, heads, _ = cache_k.shape
    n_pages = page_table.shape[1]
    groups = n_pages // pages_per_step
    assert LANES % heads == 0 and hd == LANES

    def page_spec(i):
        def index(b, g, pt):
            return (layer, pt[b, (groups - 1 - g) * pages_per_step + i], 0, 0, 0)
        return pl.BlockSpec((1, 1, page, heads, hd), index)

    per_sample = pl.BlockSpec((1, hp, hd), lambda b, g, pt: (b, 0, 0))
    kernel = functools.partial(_sb_sample_kernel, pages_per_step=pages_per_step, heads=heads, page=page,
                               past=n_pages * page, scale=hd ** -0.5)
    grid_spec = pltpu.PrefetchScalarGridSpec(
        num_scalar_prefetch=1,
        grid=(nb, groups),
        in_specs=[per_sample, pl.BlockSpec((hp, page * heads), lambda b, g, pt: (0, 0)), per_sample, per_sample]
        + [page_spec(i) for i in range(pages_per_step)] * 2,
        out_specs=pl.BlockSpec((1, hp, hd), lambda b, g, pt: (b, 0, 0)),
        scratch_shapes=[pltpu.VMEM((hp, LANES), F32), pltpu.VMEM((hp, hd), F32)],
    )
    return pl.pallas_call(
        kernel,
        grid_spec=grid_spec,
        out_shape=jax.ShapeDtypeStruct((nb, hp, hd), F32),
        compiler_params=_params("parallel", "arbitrary"),
        name="sb_sample",
    )(page_table, q, bias_b, k_new, v_new, *([cache_k] * pages_per_step), *([cache_v] * pages_per_step))


def _hgrn_sample_kernel(q_ref, f_ref, i_ref, g_ref, lb_ref, nw_ref, s_ref, o_ref, so_ref, *, samples):
    h = pl.program_id(0)
    kd = q_ref.shape[-1]
    lb = lb_ref[pl.ds(h, 1), :]
    qh, kh, logf = _hgrn_gates(q_ref[...], f_ref[...], lb, kd)
    vh = i_ref[...]
    f = jnp.exp(logf)
    qk = jnp.sum(qh * kh, axis=-1, keepdims=True)
    o_ref[...] = jnp.zeros_like(o_ref)
    for b in range(samples):
        row = slice(b, b + 1)
        state = s_ref[b, 0]
        f_col = jnp.broadcast_to(f[row], (kd, kd)).T
        k_col = jnp.broadcast_to(kh[row], (kd, kd)).T
        so_ref[b, 0] = f_col * state + k_col * vh[row]
        q_dec = jnp.broadcast_to(qh[row] * f[row], (SUBLANES, kd)).astype(BF16)
        o = _dot(q_dec, state.astype(BF16))[0:1] + qk[row] * vh[row]
        o_ref[row, :] = _hgrn_out(o, nw_ref[pl.ds(h, 1), :], g_ref[row, :])


def hgrn_sample(proj, lb, norm_w, state, *, layer, heads, key_dim, col0, samples):
    rows = proj.shape[0]

    def seg(k):
        return pl.BlockSpec((rows, key_dim), lambda h: (0, col0 + k * heads + h))

    whole = pl.BlockSpec((heads, key_dim), lambda h: (0, 0))
    return pl.pallas_call(
        functools.partial(_hgrn_sample_kernel, samples=samples),
        grid=(heads,),
        in_specs=[seg(0), seg(1), seg(2), seg(3), whole, whole,
                  pl.BlockSpec((None, samples, 1, key_dim, key_dim), lambda h: (layer, 0, h, 0, 0))],
        out_specs=[
            pl.BlockSpec((rows, key_dim), lambda h: (0, h)),
            pl.BlockSpec((samples, 1, key_dim, key_dim), lambda h: (0, h, 0, 0)),
        ],
        out_shape=[
            jax.ShapeDtypeStruct((rows, heads * key_dim), F32),
            jax.ShapeDtypeStruct((samples, heads, key_dim, key_dim), F32),
        ],
        compiler_params=_params("parallel"),
        name="hgrn_sample",
    )(proj, proj, proj, proj, lb, norm_w, state)


SAMPLE_ROWS = 16
TILES = dict(in_tm=1024, in_tn=512, sb_tq=512, sb_tk=256, hg_chunk=256, hg_heads=4, merge_tm=256, mlp_tm=512,
             mlp_tf=512, sample_pages=8)


def kernel(x_prompt, x_sample, cache_k, cache_v, state_hgrn, page_table, norm_mix_pre, norm_mix_post,
           norm_mlp_pre, norm_mlp_post, w_in, sb_bias, lower_bounds, hgrn_norm, w_branch_a, w_branch_b, w_out,
           w_up, w_down):
    batch, seq, d_model = x_prompt.shape
    nb, dec_seq, _ = x_sample.shape
    depth, _, page, heads, head_dim = cache_k.shape
    assert dec_seq == 1
    sb_width = heads * head_dim
    key_width = lower_bounds.shape[1]
    key_dim = key_width // heads
    hg_col0 = 3 * sb_width // key_dim
    gate_off = 3 * sb_width + 4 * key_width
    t = TILES

    lb_all = jnp.cumsum(jax.nn.softmax(lower_bounds.astype(F32), axis=0), axis=0)
    lb_all = jnp.maximum(lb_all - lb_all[:1], 0.0).reshape(depth, heads, key_dim)
    norm_w = hgrn_norm.reshape(depth, heads, key_dim)
    w_a_b, w_b_b, w_o_b, w_up_b, w_down_b = (w.astype(BF16) for w in (w_branch_a, w_branch_b, w_out, w_up, w_down))

    xp = x_prompt.reshape(batch * seq, d_model)
    xs = jnp.pad(x_sample.reshape(nb, d_model), ((0, SAMPLE_ROWS - nb), (0, 0)))
    pad_heads = lambda a: jnp.pad(a.reshape(nb, heads, head_dim), ((0, 0), (0, SAMPLE_ROWS - heads), (0, 0)))
    s_prompt, k_sample, v_sample, s_sample = [], [], [], []
    k_buf = v_buf = None
    for l in range(depth):
        proj, proj_s, k_buf, v_buf = norm_matmul(xp, xs, norm_mix_pre[l], w_in, layer=l, tm=t["in_tm"], tn=t["in_tn"],
                                                 col0=sb_width, width=sb_width, k_buf=k_buf, v_buf=v_buf)
        o_a = sb_prompt(proj, k_buf, v_buf, sb_bias[l], layer=l, batch=batch, seq=seq, heads=heads,
                        head_dim=head_dim, tq=t["sb_tq"], tk=t["sb_tk"])
        o_b, s_p = hgrn_prompt(proj, lb_all[l], norm_w[l], batch=batch, seq=seq, heads=heads, key_dim=key_dim,
                               col0=hg_col0, chunk=t["hg_chunk"], hps=t["hg_heads"])
        k_s = proj_s[:nb, sb_width:2 * sb_width]
        v_s = proj_s[:nb, 2 * sb_width:3 * sb_width]
        bias_b = jnp.pad(jnp.broadcast_to(sb_bias[l][:, None], (heads, page * heads)),
                         ((0, SAMPLE_ROWS - heads), (0, 0)))
        o_as = sb_sample(pad_heads(proj_s[:nb, :sb_width]), bias_b, pad_heads(k_s), pad_heads(v_s), cache_k, cache_v,
                         page_table, layer=l, pages_per_step=t["sample_pages"])
        o_as = jnp.pad(o_as[:, :heads].reshape(nb, sb_width), ((0, SAMPLE_ROWS - nb), (0, 0)))
        o_bs, s_s = hgrn_sample(proj_s, lb_all[l], norm_w[l], state_hgrn, layer=l, heads=heads, key_dim=key_dim,
                                col0=hg_col0, samples=nb)
        xp, xs = merge_out((o_a, o_b, proj, xp), (o_as, o_bs, proj_s, xs), w_a_b, w_b_b, w_o_b, norm_mix_post[l],
                           layer=l, tm=t["merge_tm"], gate_off=gate_off)
        xp, xs = mlp(xp, xs, norm_mlp_pre[l], w_up_b, w_down_b, norm_mlp_post[l], layer=l, tm=t["mlp_tm"],
                     tf=t["mlp_tf"])
        s_prompt.append(s_p)
        k_sample.append(k_s.reshape(nb, dec_seq, heads, head_dim))
        v_sample.append(v_s.reshape(nb, dec_seq, heads, head_dim))
        s_sample.append(s_s)

    k_p, v_p = (buf.reshape(depth, batch, seq, heads, head_dim) for buf in (k_buf, v_buf))
    return (xp.reshape(batch, seq, d_model), xs[:nb].reshape(nb, dec_seq, d_model), k_p, v_p, jnp.stack(s_prompt),
            jnp.stack(k_sample), jnp.stack(v_sample), jnp.stack(s_sample))
```

```python
import functools

import numpy as np
import jax
import jax.numpy as jnp
from jax import lax
from jax.experimental import pallas as pl
from jax.experimental.pallas import tpu as pltpu

F32 = jnp.float32
BF16 = jnp.bfloat16
RMS_EPS = 1e-6
LANES = 128
SUBLANES = 8
VMEM_LIMIT = 56 * 1024 * 1024

NT_DIMS = (((1,), (1,)), ((), ()))


def _dot(a, b):
    return jnp.dot(a, b, preferred_element_type=F32)


def _dot_nt(a, b):
    return lax.dot_general(a, b, NT_DIMS, preferred_element_type=F32)


def _sigmoid(x):
    return 1.0 / (1.0 + jnp.exp(-x))


def _softplus(x):
    return jnp.maximum(x, 0.0) + jnp.log(1.0 + jnp.exp(-jnp.abs(x)))


def _rms_scale(x, w):
    return x * lax.rsqrt(jnp.mean(x * x, axis=-1, keepdims=True) + RMS_EPS) * w


def _split_bf16(x, parts):
    out = []
    for _ in range(parts - 1):
        hi = x.astype(BF16)
        out.append(hi)
        x = x - hi.astype(F32)
    out.append(x.astype(BF16))
    return out


def _params(*sem):
    return pltpu.CompilerParams(dimension_semantics=sem, vmem_limit_bytes=VMEM_LIMIT)


def _norm_matmul_kernel(x_ref, xs_ref, g_ref, w_ref, *rest, kv_tiles):
    o_ref, os_ref, k_ref, v_ref, xn_ref = rest[-5:]
    j = pl.program_id(1)
    tm = x_ref.shape[0]

    @pl.when(j == 0)
    def _():
        xn_ref[:tm, :] = _rms_scale(x_ref[...], g_ref[...]).astype(BF16)
        xn_ref[tm:, :] = _rms_scale(xs_ref[...], g_ref[...]).astype(BF16)

    both = _dot(xn_ref[...], w_ref[...].astype(BF16))
    res = both[:tm]
    o_ref[...] = res
    os_ref[...] = both[tm:]
    k0, v0, n = kv_tiles

    @pl.when((j >= k0) & (j < k0 + n))
    def _():
        k_ref[...] = res

    @pl.when((j >= v0) & (j < v0 + n))
    def _():
        v_ref[...] = res


def norm_matmul(x, xs, gain, w_all, *, layer, tm, tn, col0, width, k_buf, v_buf):
    m, d = x.shape
    rows_s = xs.shape[0]
    depth, _, n = w_all.shape
    in_specs = [
        pl.BlockSpec((tm, d), lambda i, j: (i, 0)),
        pl.BlockSpec((rows_s, d), lambda i, j: (0, 0)),
        pl.BlockSpec((1, d), lambda i, j: (0, 0)),
        pl.BlockSpec((None, d, tn), lambda i, j: (layer, 0, j)),
    ]
    out_specs = [pl.BlockSpec((tm, tn), lambda i, j: (i, j)), pl.BlockSpec((None, rows_s, tn), lambda i, j: (i, 0, j))]
    out_shape = [jax.ShapeDtypeStruct((m, n), F32), jax.ShapeDtypeStruct((m // tm, rows_s, n), F32)]
    args = [x, xs, gain.reshape(1, d), w_all]
    nt = width // tn
    kv_tiles = (col0 // tn, (col0 + width) // tn, nt)
    for first in kv_tiles[:2]:
        index = functools.partial(lambda i, j, f: (layer, i, jnp.clip(j - f, 0, nt - 1)), f=first)
        out_specs.append(pl.BlockSpec((None, tm, tn), index))
        out_shape.append(jax.ShapeDtypeStruct((depth, m, width), F32))
    aliases = {}
    if k_buf is not None:
        in_specs += [pl.BlockSpec(memory_space=pl.ANY)] * 2
        args += [k_buf, v_buf]
        aliases = {4: 2, 5: 3}
    proj, proj_s, k_buf, v_buf = pl.pallas_call(
        functools.partial(_norm_matmul_kernel, kv_tiles=kv_tiles),
        grid=(m // tm, n // tn),
        in_specs=in_specs,
        out_specs=out_specs,
        out_shape=out_shape,
        scratch_shapes=[pltpu.VMEM((tm + rows_s, d), BF16)],
        input_output_aliases=aliases,
        compiler_params=_params("arbitrary", "arbitrary"),
        name="norm_in_proj",
    )(*args)
    return proj, proj_s[0], k_buf, v_buf


def _sb_prompt_kernel(bias_ref, q_ref, k_ref, v_ref, o_ref, kb_ref, vb_ref, zr_ref, z_ref, s_ref, in_ref, w_ref,
                      acc_ref, *, tq, tk, scale, pieces):
    h = pl.program_id(1)
    i = pl.program_id(2)
    n_sub = tq // tk
    assert n_sub == 2

    @pl.when(i == 0)
    def _():
        kb_ref[...] = k_ref[...].astype(BF16)
        vb_ref[...] = v_ref[...].astype(BF16)

    q = (q_ref[...] * scale).astype(BF16)
    bias = bias_ref[h]
    row = lax.broadcasted_iota(jnp.int32, (pieces * tk, tk), 0) % tk
    col = lax.broadcasted_iota(jnp.int32, (pieces * tk, tk), 1)
    neg_suffix = jnp.where(row >= col, -1.0, 0.0).astype(BF16)
    local_q = lax.broadcasted_iota(jnp.int32, (tq, tk), 0)
    local_k = lax.broadcasted_iota(jnp.int32, (tq, tk), 1)

    nblk = (i + 1) * n_sub
    ks_of = lambda n: pl.multiple_of((nblk - 1 - n) * tk, tk)
    mask_of = lambda n: (local_k + (tk if n == 0 else 0)) < local_q

    def st_scores(slot, n):
        zr_ref[slot] = _dot_nt(q, kb_ref[pl.ds(ks_of(n), tk), :]) + bias

    def st_softplus(slot, later, n=None):
        z = zr_ref[slot]
        sp = _softplus(z)
        if n is not None:
            sp = jnp.where(mask_of(n), sp, 0.0)
        z_ref[slot] = z + later
        s_ref[slot] = jnp.concatenate(_split_bf16(sp, pieces), axis=1)
        return later - jnp.sum(sp, axis=-1, keepdims=True)

    def st_suffix(slot):
        in_ref[slot] = _dot(s_ref[slot], neg_suffix)

    def st_weights(slot, n=None):
        w = jnp.exp(z_ref[slot] + in_ref[slot])
        if n is not None:
            w = jnp.where(mask_of(n), w, 0.0)
        w_ref[slot] = w.astype(BF16)

    def st_values(slot, n, acc):
        return acc + _dot(w_ref[slot], vb_ref[pl.ds(ks_of(n), tk), :])

    zero_later = jnp.zeros((tq, 1), F32)
    zero_acc = jnp.zeros((tq, LANES), F32)

    @pl.when(i < 2)
    def _():
        def block(n, later, acc, masked):
            st_scores(0, n)
            later = st_softplus(0, later, n if masked else None)
            st_suffix(0)
            st_weights(0, n if masked else None)
            return later, st_values(0, n, acc)

        carry = block(1, *block(0, zero_later, zero_acc, True), True)
        _, acc = lax.fori_loop(n_sub, nblk, lambda n, c: block(n, *c, False), carry)
        acc_ref[...] = acc

    @pl.when(i >= 2)
    def _():
        def trip(n, p, later, acc, first=-1, last=4):
            static = isinstance(n, int)
            if first < 0 <= last:
                acc = st_values(p, n, acc)
            if first < 1 <= last:
                st_weights(1 - p, n + 1 if static and n + 1 < n_sub else None)
            if first < 2 <= last:
                st_suffix(p)
            if first < 3 <= last:
                later = st_softplus(1 - p, later, n + 3 if static and n + 3 < n_sub else None)
            if first < 4 <= last:
                st_scores(p, n + 4)
            return later, acc

        c = (zero_later, zero_acc)
        for n in range(-4, 2):
            c = trip(n, n % 2, *c, first=-n - 1)

        def two_trips(pair, c):
            n = 2 * pair + 2
            return trip(n + 1, 1, *trip(n, 0, *c))

        c = lax.fori_loop(0, i - 2, two_trips, c)
        for d in range(4):
            c = trip(nblk - 4 + d, d % 2, *c, last=3 - d)
        acc_ref[...] = c[1]

    o_ref[...] = acc_ref[...].astype(o_ref.dtype)


def sb_prompt(proj, k_buf, v_buf, bias, *, layer, batch, seq, heads, head_dim, tq, tk, pieces):
    assert head_dim == LANES
    nq = seq // tq
    kernel = functools.partial(_sb_prompt_kernel, tq=tq, tk=tk, scale=head_dim ** -0.5, pieces=pieces)
    kv_spec = pl.BlockSpec((None, seq, head_dim), lambda b, h, i: (layer, b, h))
    return pl.pallas_call(
        kernel,
        grid=(batch, heads, nq),
        in_specs=[
            pl.BlockSpec(memory_space=pltpu.SMEM),
            pl.BlockSpec((tq, head_dim), lambda b, h, i: (b * nq + i, h)),
            kv_spec, kv_spec,
        ],
        out_specs=pl.BlockSpec((tq, head_dim), lambda b, h, i: (b * nq + i, h)),
        out_shape=jax.ShapeDtypeStruct((batch * seq, heads * head_dim), BF16),
        scratch_shapes=[
            pltpu.VMEM((seq, head_dim), BF16), pltpu.VMEM((seq, head_dim), BF16),
            pltpu.VMEM((2, tq, tk), F32), pltpu.VMEM((2, tq, tk), F32), pltpu.VMEM((2, tq, pieces * tk), BF16),
            pltpu.VMEM((2, tq, tk), F32), pltpu.VMEM((2, tq, tk), BF16), pltpu.VMEM((tq, head_dim), F32),
        ],
        compiler_params=_params("parallel", "parallel", "arbitrary"),
        name="sb_prompt",
    )(bias, proj, k_buf, v_buf)


def _hgrn_gates(qb, fpre, lb, key_dim):
    e = jnp.exp(-jnp.abs(fpre))
    r = 1.0 / (1.0 + e)
    log_sig = jnp.minimum(fpre, 0.0) + jnp.log(r)
    sig_neg = jnp.where(fpre >= 0.0, e * r, r)
    a = jnp.log(lb)
    b = jnp.log(1.0 - lb) + log_sig
    logf = jnp.maximum(a, b) + jnp.log(1.0 + jnp.exp(-jnp.abs(a - b)))
    kh = (1.0 - lb) * sig_neg
    qh = qb * _sigmoid(qb) * (key_dim ** -0.5)
    return qh, kh, logf


def _hgrn_out(o, norm_w, gate):
    return _rms_scale(o, norm_w) * (gate * _sigmoid(gate))


def _level_mask_table(chunk):
    t = np.arange(chunk)[:, None]
    s = np.arange(chunk)[None, :]
    table = np.full((chunk, chunk), -1, np.int32)
    m, lvl = SUBLANES, 0
    while m < chunk:
        sibling = (t // (2 * m) == s // (2 * m)) & (t % (2 * m) >= m) & (s % (2 * m) < m)
        table[sibling] = lvl
        m, lvl = 2 * m, lvl + 1
    return table


def _hgrn_prompt_kernel(q_ref, f_ref, i_ref, g_ref, lb_ref, nw_ref, lm_ref, o_ref, s_ref, state_ref, *, chunk, hps):
    hb = pl.program_id(1)
    ci = pl.program_id(2)
    kd = q_ref.shape[-1] // hps

    @pl.when(ci == 0)
    def _():
        state_ref[...] = jnp.zeros_like(state_ref)

    row = lax.broadcasted_iota(jnp.int32, (chunk, chunk), 0)
    col = lax.broadcasted_iota(jnp.int32, (chunk, chunk), 1)
    prefix_incl = jnp.where(col <= row, 1.0, 0.0).astype(BF16)
    rows = lax.broadcasted_iota(jnp.int32, (chunk, kd), 0)
    lm = lm_ref[...]
    groups = chunk // SUBLANES
    sub = lax.broadcasted_iota(jnp.int32, (groups, SUBLANES, kd), 1)

    for hh in range(hps):
        lanes = slice(hh * kd, (hh + 1) * kd)
        lb = lb_ref[pl.ds(hb * hps + hh, 1), :]
        qh, kh, logf = _hgrn_gates(q_ref[:, lanes], f_ref[:, lanes], lb, kd)
        vh = i_ref[:, lanes]
        vh_b = vh.astype(BF16)

        b = sum(_dot(prefix_incl, part) for part in _split_bf16(logf, 3))

        state = state_ref[hh]
        o = _dot((qh * jnp.exp(b)).astype(BF16), state.astype(BF16))

        scores = jnp.zeros((chunk, chunk), F32)
        m, lvl = SUBLANES, 0
        while m < chunk:
            bnd = jnp.concatenate(
                [jnp.broadcast_to(b[p + m - 1:p + m, :], (2 * m, kd)) for p in range(0, chunk, 2 * m)], axis=0)
            second = (rows & (2 * m - 1)) >= m
            d = b - bnd
            qt = jnp.where(second, qh * jnp.exp(d), 0.0)
            kt = jnp.where(second, 0.0, kh * jnp.exp(-d))
            scores = jnp.where(lm == lvl, _dot_nt(qt.astype(BF16), kt.astype(BF16)), scores)
            m, lvl = 2 * m, lvl + 1
        o = o + _dot(scores.astype(BF16), vh_b)

        q3 = qh.reshape(groups, SUBLANES, kd)
        k3 = kh.reshape(groups, SUBLANES, kd)
        b3 = b.reshape(groups, SUBLANES, kd)
        v3 = vh.reshape(groups, SUBLANES, kd)
        o3 = jnp.zeros((groups, SUBLANES, kd), F32)
        for s in range(SUBLANES):
            decay = jnp.exp(b3 - b3[:, s:s + 1, :])
            a = jnp.sum(q3 * k3[:, s:s + 1, :] * decay, axis=-1, keepdims=True)
            o3 = o3 + jnp.where(sub >= s, a, 0.0) * v3[:, s:s + 1, :]
        o = o + o3.reshape(chunk, kd)

        b_last = b[chunk - 1:chunk, :]
        k_dec = kh * jnp.exp(b_last - b)
        decay_col = jnp.broadcast_to(jnp.exp(b_last), (kd, kd)).T
        state_ref[hh] = decay_col * state + _dot(k_dec.T.astype(BF16), vh_b)

        o_ref[:, lanes] = _hgrn_out(o, nw_ref[pl.ds(hb * hps + hh, 1), :], g_ref[:, lanes]).astype(o_ref.dtype)

    @pl.when(ci == pl.num_programs(2) - 1)
    def _():
        s_ref[0] = state_ref[...]


def hgrn_prompt(proj, lb, norm_w, *, batch, seq, heads, key_dim, col0, chunk, hps):
    assert key_dim == LANES and heads % hps == 0 and col0 % hps == 0
    nc = seq // chunk
    lm = jnp.asarray(_level_mask_table(chunk))

    def seg(k):
        return pl.BlockSpec((chunk, hps * key_dim), lambda b, h, c: (b * nc + c, (col0 + k * heads) // hps + h))

    whole = lambda shape: pl.BlockSpec(shape, lambda b, h, c: (0,) * len(shape))
    return pl.pallas_call(
        functools.partial(_hgrn_prompt_kernel, chunk=chunk, hps=hps),
        grid=(batch, heads // hps, nc),
        in_specs=[seg(0), seg(1), seg(2), seg(3), whole((heads, key_dim)), whole((heads, key_dim)),
                  whole((chunk, chunk))],
        out_specs=[
            pl.BlockSpec((chunk, hps * key_dim), lambda b, h, c: (b * nc + c, h)),
            pl.BlockSpec((1, hps, key_dim, key_dim), lambda b, h, c: (b, h, 0, 0)),
        ],
        out_shape=[
            jax.ShapeDtypeStruct((batch * seq, heads * key_dim), BF16),
            jax.ShapeDtypeStruct((batch, heads, key_dim, key_dim), F32),
        ],
        scratch_shapes=[pltpu.VMEM((hps, key_dim, key_dim), F32)],
        compiler_params=_params("parallel", "parallel", "arbitrary"),
        name="hgrn_prompt",
    )(proj, proj, proj, proj, lb, norm_w, lm)


def _merge_out_kernel(*refs, gate_blocks):
    n_in = 3 + 2 * gate_blocks
    prompt, sample = refs[:n_in], refs[n_in:2 * n_in]
    wa_ref, wb_ref, wo_ref, g_ref, o_ref, os_ref = refs[2 * n_in:]

    rows = lambda k, dtype=F32: jnp.concatenate([prompt[k][...].astype(dtype), sample[k][...].astype(dtype)], axis=0)
    gate = lambda ks: jnp.concatenate([_sigmoid(rows(k)) for k in ks], axis=1)
    a = _dot(rows(0, BF16), wa_ref[...])
    b = _dot(rows(1, BF16), wb_ref[...])
    merged = gate(range(2, 2 + gate_blocks)) * a + gate(range(2 + gate_blocks, 2 + 2 * gate_blocks)) * b
    y = _dot(merged.astype(BF16), wo_ref[...])
    out = rows(n_in - 1) + _rms_scale(y, g_ref[...])
    tm = o_ref.shape[0]
    o_ref[...] = out[:tm]
    os_ref[...] = out[tm:]


def merge_out(prompt, sample, wa_all, wb_all, wo_all, gain, *, layer, tm, gate_off):
    m, d = prompt[3].shape
    rows_s = sample[3].shape[0]
    wa_rows, wb_rows = wa_all.shape[1], wb_all.shape[1]
    gw = int(np.gcd(gate_off, d))
    gate_blocks = d // gw
    const = lambda *shape: pl.BlockSpec((None, *shape), lambda i: (layer, 0, 0), pipeline_mode=pl.Buffered(1))

    def group(rows, row_index):
        at = lambda c: (lambda i: (row_index(i), c))
        return ([pl.BlockSpec((rows, wa_rows), at(0)), pl.BlockSpec((rows, wb_rows), at(0))]
                + [pl.BlockSpec((rows, gw), at(gate_off // gw + c)) for c in range(2 * gate_blocks)]
                + [pl.BlockSpec((rows, d), at(0))])

    operands = lambda t: (t[0], t[1], *([t[2]] * (2 * gate_blocks)), t[3])
    return pl.pallas_call(
        functools.partial(_merge_out_kernel, gate_blocks=gate_blocks),
        grid=(m // tm,),
        in_specs=group(tm, lambda i: i) + group(rows_s, lambda i: 0)
        + [const(wa_rows, d), const(wb_rows, d), const(d, d), pl.BlockSpec((1, d), lambda i: (0, 0))],
        out_specs=[pl.BlockSpec((tm, d), lambda i: (i, 0)), pl.BlockSpec((rows_s, d), lambda i: (0, 0))],
        out_shape=[jax.ShapeDtypeStruct((m, d), F32), jax.ShapeDtypeStruct((rows_s, d), F32)],
        compiler_params=_params("arbitrary"),
        name="merge_out_proj",
    )(*operands(prompt), *operands(sample), wa_all, wb_all, wo_all, gain.reshape(1, d))


def _mlp_kernel(x_ref, xs_ref, gpre_ref, wu_ref, wd_ref, gpost_ref, o_ref, os_ref, hn_ref, acc_ref):
    j = pl.program_id(1)
    tm = x_ref.shape[0]

    @pl.when(j == 0)
    def _():
        hn_ref[:tm, :] = _rms_scale(x_ref[...], gpre_ref[...]).astype(BF16)
        hn_ref[tm:, :] = _rms_scale(xs_ref[...], gpre_ref[...]).astype(BF16)
        acc_ref[...] = jnp.zeros_like(acc_ref)

    u = jnp.maximum(_dot(hn_ref[...], wu_ref[...]), 0.0)
    acc_ref[...] += _dot((u * u).astype(BF16), wd_ref[...])

    @pl.when(j == pl.num_programs(1) - 1)
    def _():
        o_ref[...] = x_ref[...] + _rms_scale(acc_ref[:tm, :], gpost_ref[...])
        os_ref[...] = xs_ref[...] + _rms_scale(acc_ref[tm:, :], gpost_ref[...])


def mlp(x, xs, gpre, wu_all, wd_all, gpost, *, layer, tm, tf):
    m, d = x.shape
    rows_s = xs.shape[0]
    f = wu_all.shape[2]
    vec = pl.BlockSpec((1, d), lambda i, j: (0, 0))
    return pl.pallas_call(
        _mlp_kernel,
        grid=(m // tm, f // tf),
        in_specs=[
            pl.BlockSpec((tm, d), lambda i, j: (i, 0)),
            pl.BlockSpec((rows_s, d), lambda i, j: (0, 0)),
            vec,
            pl.BlockSpec((None, d, tf), lambda i, j: (layer, 0, j)),
            pl.BlockSpec((None, tf, d), lambda i, j: (layer, j, 0)),
            vec,
        ],
        out_specs=[pl.BlockSpec((tm, d), lambda i, j: (i, 0)), pl.BlockSpec((rows_s, d), lambda i, j: (0, 0))],
        out_shape=[jax.ShapeDtypeStruct((m, d), F32), jax.ShapeDtypeStruct((rows_s, d), F32)],
        scratch_shapes=[pltpu.VMEM((tm + rows_s, d), BF16), pltpu.VMEM((tm + rows_s, d), F32)],
        compiler_params=_params("arbitrary", "arbitrary"),
        name="mlp",
    )(x, xs, gpre.reshape(1, d), wu_all, wd_all, gpost.reshape(1, d))


def _sb_sample_kernel(pt_ref, q_ref, bias_ref, kn_ref, vn_ref, *rest, pages_per_step, heads, page, past, scale):
    del pt_ref
    k_refs = rest[:pages_per_step]
    v_refs = rest[pages_per_step:2 * pages_per_step]
    o_ref, later_ref, acc_ref = rest[2 * pages_per_step:]
    g = pl.program_id(1)
    hp, hd = q_ref.shape[1], q_ref.shape[2]
    width = page * heads
    nblk = width // LANES

    qs = q_ref[0] * scale
    q = qs.astype(BF16)

    @pl.when(g == 0)
    def _():
        z = jnp.sum(qs * kn_ref[0], axis=-1, keepdims=True) + bias_ref[:, 0:1]
        k_pos = past + lax.broadcasted_iota(jnp.int32, (hp, 1), 1)
        mask = k_pos < past
        sp = _softplus(z)
        later_ref[...] = jnp.broadcast_to(jnp.where(mask, -sp, 0.0), (hp, LANES))
        acc_ref[...] = jnp.where(mask, jnp.exp(z - sp), 0.0) * vn_ref[0]

    head_row = lax.broadcasted_iota(jnp.int32, (hp, width), 0)
    head_lane = lax.broadcasted_iota(jnp.int32, (hp, width), 1) % heads
    own = head_row == head_lane
    r = lax.broadcasted_iota(jnp.int32, (LANES, 2 * LANES), 0)
    c = lax.broadcasted_iota(jnp.int32, (LANES, 2 * LANES), 1)
    neg_sum = jnp.where((c >= LANES) | ((r % heads == c % heads) & (r // heads >= c // heads)), -1.0, 0.0).astype(BF16)
    bias = bias_ref[...]

    order = list(reversed(range(pages_per_step)))
    zs = [_dot_nt(q, k_refs[i][0, 0].reshape(width, hd).astype(BF16)) + bias for i in order]
    sps = [jnp.where(own, _softplus(z), 0.0) for z in zs]
    stacked = jnp.concatenate(
        [sp[:, b * LANES:(b + 1) * LANES] for sp in sps for b in reversed(range(nblk))], axis=0)
    s_hi, s_lo = _split_bf16(stacked, 2)
    sums = _dot(s_hi, neg_sum) + _dot(s_lo, neg_sum)

    later = later_ref[...]
    acc = acc_ref[...]
    for n, i in enumerate(order):
        pieces = [None] * nblk
        for m, b in enumerate(reversed(range(nblk))):
            rows = slice((n * nblk + m) * hp, (n * nblk + m + 1) * hp)
            blk = slice(b * LANES, (b + 1) * LANES)
            pieces[b] = jnp.where(own[:, blk], jnp.exp(zs[n][:, blk] + sums[rows, :LANES] + later), 0.0)
            later = later + sums[rows, LANES:]
        w = jnp.concatenate(pieces, axis=1).astype(BF16)
        acc = acc + _dot(w, v_refs[i][0, 0].reshape(width, hd).astype(BF16))
    later_ref[...] = later
    acc_ref[...] = acc

    @pl.when(g == pl.num_programs(1) - 1)
    def _():
        o_ref[0] = acc


def sb_sample(q, bias_b, k_new, v_new, cache_k, cache_v, page_table, *, layer, pages_per_step):
    nb, hp, hd = q.shape
    _, _, page, heads, _ = cache_k.shape
    n_pages = page_table.shape[1]
    groups = n_pages // pages_per_step
    assert LANES % heads == 0 and hd == LANES

    def page_spec(i):
        def index(b, g, pt):
            return (layer, pt[b, (groups - 1 - g) * pages_per_step + i], 0, 0, 0)
        return pl.BlockSpec((1, 1, page, heads, hd), index)

    per_sample = pl.BlockSpec((1, hp, hd), lambda b, g, pt: (b, 0, 0))
    kernel = functools.partial(_sb_sample_kernel, pages_per_step=pages_per_step, heads=heads, page=page,
                               past=n_pages * page, scale=hd ** -0.5)
    grid_spec = pltpu.PrefetchScalarGridSpec(
        num_scalar_prefetch=1,
        grid=(nb, groups),
        in_specs=[per_sample, pl.BlockSpec((hp, page * heads), lambda b, g, pt: (0, 0)), per_sample, per_sample]
        + [page_spec(i) for i in range(pages_per_step)] * 2,
        out_specs=pl.BlockSpec((1, hp, hd), lambda b, g, pt: (b, 0, 0)),
        scratch_shapes=[pltpu.VMEM((hp, LANES), F32), pltpu.VMEM((hp, hd), F32)],
    )
    return pl.pallas_call(
        kernel,
        grid_spec=grid_spec,
        out_shape=jax.ShapeDtypeStruct((nb, hp, hd), F32),
        compiler_params=_params("parallel", "arbitrary"),
        name="sb_sample",
    )(page_table, q, bias_b, k_new, v_new, *([cache_k] * pages_per_step), *([cache_v] * pages_per_step))


def _hgrn_sample_kernel(q_ref, f_ref, i_ref, g_ref, lb_ref, nw_ref, s_ref, o_ref, so_ref, *, samples):
    h = pl.program_id(0)
    kd = q_ref.shape[-1]
    lb = lb_ref[pl.ds(h, 1), :]
    qh, kh, logf = _hgrn_gates(q_ref[...], f_ref[...], lb, kd)
    vh = i_ref[...]
    f = jnp.exp(logf)
    qk = jnp.sum(qh * kh, axis=-1, keepdims=True)
    o_ref[...] = jnp.zeros_like(o_ref)
    for b in range(samples):
        row = slice(b, b + 1)
        state = s_ref[b, 0]
        f_col = jnp.broadcast_to(f[row], (kd, kd)).T
        k_col = jnp.broadcast_to(kh[row], (kd, kd)).T
        so_ref[b, 0] = f_col * state + k_col * vh[row]
        q_dec = jnp.broadcast_to(qh[row] * f[row], (SUBLANES, kd)).astype(BF16)
        o = _dot(q_dec, state.astype(BF16))[0:1] + qk[row] * vh[row]
        o_ref[row, :] = _hgrn_out(o, nw_ref[pl.ds(h, 1), :], g_ref[row, :])


def hgrn_sample(proj, lb, norm_w, state, *, layer, heads, key_dim, col0, samples):
    rows = proj.shape[0]

    def seg(k):
        return pl.BlockSpec((rows, key_dim), lambda h: (0, col0 + k * heads + h))

    whole = pl.BlockSpec((heads, key_dim), lambda h: (0, 0))
    return pl.pallas_call(
        functools.partial(_hgrn_sample_kernel, samples=samples),
        grid=(heads,),
        in_specs=[seg(0), seg(1), seg(2), seg(3), whole, whole,
                  pl.BlockSpec((None, samples, 1, key_dim, key_dim), lambda h: (layer, 0, h, 0, 0))],
        out_specs=[
            pl.BlockSpec((rows, key_dim), lambda h: (0, h)),
            pl.BlockSpec((samples, 1, key_dim, key_dim), lambda h: (0, h, 0, 0)),
        ],
        out_shape=[
            jax.ShapeDtypeStruct((rows, heads * key_dim), F32),
            jax.ShapeDtypeStruct((samples, heads, key_dim, key_dim), F32),
        ],
        compiler_params=_params("parallel"),
        name="hgrn_sample",
    )(proj, proj, proj, proj, lb, norm_w, state)


SAMPLE_ROWS = 16
TILES = dict(in_tm=1024, in_tn=512, sb_tq=512, sb_tk=256, sb_pieces=1, hg_chunk=256, hg_heads=4, merge_tm=256, mlp_tm=512,
             mlp_tf=512, sample_pages=8)


def kernel(x_prompt, x_sample, cache_k, cache_v, state_hgrn, page_table, norm_mix_pre, norm_mix_post,
           norm_mlp_pre, norm_mlp_post, w_in, sb_bias, lower_bounds, hgrn_norm, w_branch_a, w_branch_b, w_out,
           w_up, w_down):
    batch, seq, d_model = x_prompt.shape
    nb, dec_seq, _ = x_sample.shape
    depth, _, page, heads, head_dim = cache_k.shape
    assert dec_seq == 1
    sb_width = heads * head_dim
    key_width = lower_bounds.shape[1]
    key_dim = key_width // heads
    hg_col0 = 3 * sb_width // key_dim
    gate_off = 3 * sb_width + 4 * key_width
    t = TILES

    lb_all = jnp.cumsum(jax.nn.softmax(lower_bounds.astype(F32), axis=0), axis=0)
    lb_all = jnp.maximum(lb_all - lb_all[:1], 0.0).reshape(depth, heads, key_dim)
    norm_w = hgrn_norm.reshape(depth, heads, key_dim)
    w_a_b, w_b_b, w_o_b, w_up_b, w_down_b = (w.astype(BF16) for w in (w_branch_a, w_branch_b, w_out, w_up, w_down))

    xp = x_prompt.reshape(batch * seq, d_model)
    xs = jnp.pad(x_sample.reshape(nb, d_model), ((0, SAMPLE_ROWS - nb), (0, 0)))
    pad_heads = lambda a: jnp.pad(a.reshape(nb, heads, head_dim), ((0, 0), (0, SAMPLE_ROWS - heads), (0, 0)))
    s_prompt, k_sample, v_sample, s_sample = [], [], [], []
    k_buf = v_buf = None
    for l in range(depth):
        proj, proj_s, k_buf, v_buf = norm_matmul(xp, xs, norm_mix_pre[l], w_in, layer=l, tm=t["in_tm"], tn=t["in_tn"],
                                                 col0=sb_width, width=sb_width, k_buf=k_buf, v_buf=v_buf)
        o_a = sb_prompt(proj, k_buf, v_buf, sb_bias[l], layer=l, batch=batch, seq=seq, heads=heads,
                        head_dim=head_dim, tq=t["sb_tq"], tk=t["sb_tk"], pieces=t["sb_pieces"])
        o_b, s_p = hgrn_prompt(proj, lb_all[l], norm_w[l], batch=batch, seq=seq, heads=heads, key_dim=key_dim,
                               col0=hg_col0, chunk=t["hg_chunk"], hps=t["hg_heads"])
        k_s = proj_s[:nb, sb_width:2 * sb_width]
        v_s = proj_s[:nb, 2 * sb_width:3 * sb_width]
        bias_b = jnp.pad(jnp.broadcast_to(sb_bias[l][:, None], (heads, page * heads)),
                         ((0, SAMPLE_ROWS - heads), (0, 0)))
        o_as = sb_sample(pad_heads(proj_s[:nb, :sb_width]), bias_b, pad_heads(k_s), pad_heads(v_s), cache_k, cache_v,
                         page_table, layer=l, pages_per_step=t["sample_pages"])
        o_as = jnp.pad(o_as[:, :heads].reshape(nb, sb_width), ((0, SAMPLE_ROWS - nb), (0, 0)))
        o_bs, s_s = hgrn_sample(proj_s, lb_all[l], norm_w[l], state_hgrn, layer=l, heads=heads, key_dim=key_dim,
                                col0=hg_col0, samples=nb)
        xp, xs = merge_out((o_a, o_b, proj, xp), (o_as, o_bs, proj_s, xs), w_a_b, w_b_b, w_o_b, norm_mix_post[l],
                           layer=l, tm=t["merge_tm"], gate_off=gate_off)
        xp, xs = mlp(xp, xs, norm_mlp_pre[l], w_up_b, w_down_b, norm_mlp_post[l], layer=l, tm=t["mlp_tm"],
                     tf=t["mlp_tf"])
        s_prompt.append(s_p)
        k_sample.append(k_s.reshape(nb, dec_seq, heads, head_dim))
        v_sample.append(v_s.reshape(nb, dec_seq, heads, head_dim))
        s_sample.append(s_s)

    k_p, v_p = (buf.reshape(depth, batch, seq, heads, head_dim) for buf in (k_buf, v_buf))
    return (xp.reshape(batch, seq, d_model), xs[:nb].reshape(nb, dec_seq, d_model), k_p, v_p, jnp.stack(s_prompt),
            jnp.stack(k_sample), jnp.stack(v_sample), jnp.stack(s_sample))
```

```python
import functools

import numpy as np
import jax
import jax.numpy as jnp
from jax import lax
from jax.experimental import pallas as pl
from jax.experimental.pallas import tpu as pltpu

F32 = jnp.float32
BF16 = jnp.bfloat16
RMS_EPS = 1e-6
LANES = 128
SUBLANES = 8
VMEM_LIMIT = 56 * 1024 * 1024

NT_DIMS = (((1,), (1,)), ((), ()))


def _dot(a, b):
    return jnp.dot(a, b, preferred_element_type=F32)


def _dot_nt(a, b):
    return lax.dot_general(a, b, NT_DIMS, preferred_element_type=F32)


def _sigmoid(x):
    return 1.0 / (1.0 + jnp.exp(-x))


def _softplus(x):
    return jnp.maximum(x, 0.0) + jnp.log(1.0 + jnp.exp(-jnp.abs(x)))


def _rms_scale(x, w):
    return x * lax.rsqrt(jnp.mean(x * x, axis=-1, keepdims=True) + RMS_EPS) * w


def _split_bf16(x, parts):
    out = []
    for _ in range(parts - 1):
        hi = x.astype(BF16)
        out.append(hi)
        x = x - hi.astype(F32)
    out.append(x.astype(BF16))
    return out


def _params(*sem):
    return pltpu.CompilerParams(dimension_semantics=sem, vmem_limit_bytes=VMEM_LIMIT)


def _norm_cast_kernel(x_ref, xs_ref, g_ref, o_ref, os_ref):
    o_ref[...] = _rms_scale(x_ref[...], g_ref[...]).astype(BF16)
    os_ref[...] = _rms_scale(xs_ref[...], g_ref[...]).astype(BF16)


def norm_cast(x, xs, gain, *, tm):
    m, d = x.shape
    rows_s = xs.shape[0]
    sample = pl.BlockSpec((rows_s, d), lambda i: (0, 0))
    return pl.pallas_call(
        _norm_cast_kernel,
        grid=(m // tm,),
        in_specs=[pl.BlockSpec((tm, d), lambda i: (i, 0)), sample, pl.BlockSpec((1, d), lambda i: (0, 0))],
        out_specs=[pl.BlockSpec((tm, d), lambda i: (i, 0)), sample],
        out_shape=[jax.ShapeDtypeStruct((m, d), BF16), jax.ShapeDtypeStruct((rows_s, d), BF16)],
        compiler_params=_params("arbitrary"),
        name="norm_cast",
    )(x, xs, gain.reshape(1, d))


def _in_proj_kernel(x_ref, xs_ref, w_ref, *rest, kv_tiles):
    o_ref, os_ref, k_ref, v_ref = rest[-4:]
    i = pl.program_id(0)
    j = pl.program_id(1)
    w = w_ref[...].astype(BF16)
    res = _dot(x_ref[...], w)
    o_ref[...] = res

    @pl.when(i == 0)
    def _():
        os_ref[...] = _dot(xs_ref[...], w)

    k0, v0, n = kv_tiles

    @pl.when((j >= k0) & (j < k0 + n))
    def _():
        k_ref[...] = res

    @pl.when((j >= v0) & (j < v0 + n))
    def _():
        v_ref[...] = res


def in_proj(x, xs, w_all, *, layer, tm, tn, col0, width, k_buf, v_buf):
    m, d = x.shape
    rows_s = xs.shape[0]
    depth, _, n = w_all.shape
    nj = n // tn
    in_specs = [
        pl.BlockSpec((tm, d), lambda i, j: (i, 0)),
        pl.BlockSpec((rows_s, d), lambda i, j: (0, 0)),
        pl.BlockSpec((None, d, tn), lambda i, j: (layer, 0, j)),
    ]
    out_specs = [pl.BlockSpec((tm, tn), lambda i, j: (i, j)),
                 pl.BlockSpec((rows_s, tn), lambda i, j: (0, jnp.where(i == 0, j, nj - 1)))]
    out_shape = [jax.ShapeDtypeStruct((m, n), F32), jax.ShapeDtypeStruct((rows_s, n), F32)]
    args = [x, xs, w_all]
    nt = width // tn
    kv_tiles = (col0 // tn, (col0 + width) // tn, nt)
    for first in kv_tiles[:2]:
        index = functools.partial(lambda i, j, f: (layer, i, jnp.clip(j - f, 0, nt - 1)), f=first)
        out_specs.append(pl.BlockSpec((None, tm, tn), index))
        out_shape.append(jax.ShapeDtypeStruct((depth, m, width), F32))
    aliases = {}
    if k_buf is not None:
        in_specs += [pl.BlockSpec(memory_space=pl.ANY)] * 2
        args += [k_buf, v_buf]
        aliases = {3: 2, 4: 3}
    return pl.pallas_call(
        functools.partial(_in_proj_kernel, kv_tiles=kv_tiles),
        grid=(m // tm, nj),
        in_specs=in_specs,
        out_specs=out_specs,
        out_shape=out_shape,
        input_output_aliases=aliases,
        compiler_params=_params("arbitrary", "arbitrary"),
        name="in_proj",
    )(*args)


def _sb_prompt_kernel(bias_ref, q_ref, k_ref, v_ref, o_ref, kb_ref, vb_ref, zr_ref, z_ref, s_ref, in_ref, w_ref,
                      acc_ref, *, tq, tk, scale, pieces):
    h = pl.program_id(1)
    i = pl.program_id(2)
    n_sub = tq // tk
    assert n_sub == 2

    @pl.when(i == 0)
    def _():
        kb_ref[...] = k_ref[...].astype(BF16)
        vb_ref[...] = v_ref[...].astype(BF16)

    q = (q_ref[...] * scale).astype(BF16)
    bias = bias_ref[h]
    row = lax.broadcasted_iota(jnp.int32, (pieces * tk, tk), 0) % tk
    col = lax.broadcasted_iota(jnp.int32, (pieces * tk, tk), 1)
    neg_suffix = jnp.where(row >= col, -1.0, 0.0).astype(BF16)
    local_q = lax.broadcasted_iota(jnp.int32, (tq, tk), 0)
    local_k = lax.broadcasted_iota(jnp.int32, (tq, tk), 1)

    nblk = (i + 1) * n_sub
    ks_of = lambda n: pl.multiple_of((nblk - 1 - n) * tk, tk)
    mask_of = lambda n: (local_k + (tk if n == 0 else 0)) < local_q

    def st_scores(slot, n):
        zr_ref[slot] = _dot_nt(q, kb_ref[pl.ds(ks_of(n), tk), :]) + bias

    def st_softplus(slot, later, n=None):
        z = zr_ref[slot]
        sp = _softplus(z)
        if n is not None:
            sp = jnp.where(mask_of(n), sp, 0.0)
        z_ref[slot] = z + later
        s_ref[slot] = jnp.concatenate(_split_bf16(sp, pieces), axis=1)
        return later - jnp.sum(sp, axis=-1, keepdims=True)

    def st_suffix(slot):
        in_ref[slot] = _dot(s_ref[slot], neg_suffix)

    def st_weights(slot, n=None):
        w = jnp.exp(z_ref[slot] + in_ref[slot])
        if n is not None:
            w = jnp.where(mask_of(n), w, 0.0)
        w_ref[slot] = w.astype(BF16)

    def st_values(slot, n, acc):
        return acc + _dot(w_ref[slot], vb_ref[pl.ds(ks_of(n), tk), :])

    zero_later = jnp.zeros((tq, 1), F32)
    zero_acc = jnp.zeros((tq, LANES), F32)

    @pl.when(i < 2)
    def _():
        def block(n, later, acc, masked):
            st_scores(0, n)
            later = st_softplus(0, later, n if masked else None)
            st_suffix(0)
            st_weights(0, n if masked else None)
            return later, st_values(0, n, acc)

        carry = block(1, *block(0, zero_later, zero_acc, True), True)
        _, acc = lax.fori_loop(n_sub, nblk, lambda n, c: block(n, *c, False), carry)
        acc_ref[...] = acc

    @pl.when(i >= 2)
    def _():
        def trip(n, p, later, acc, first=-1, last=4):
            static = isinstance(n, int)
            if first < 0 <= last:
                acc = st_values(p, n, acc)
            if first < 1 <= last:
                st_weights(1 - p, n + 1 if static and n + 1 < n_sub else None)
            if first < 2 <= last:
                st_suffix(p)
            if first < 3 <= last:
                later = st_softplus(1 - p, later, n + 3 if static and n + 3 < n_sub else None)
            if first < 4 <= last:
                st_scores(p, n + 4)
            return later, acc

        c = (zero_later, zero_acc)
        for n in range(-4, 2):
            c = trip(n, n % 2, *c, first=-n - 1)

        def two_trips(pair, c):
            n = 2 * pair + 2
            return trip(n + 1, 1, *trip(n, 0, *c))

        c = lax.fori_loop(0, i - 2, two_trips, c)
        for d in range(4):
            c = trip(nblk - 4 + d, d % 2, *c, last=3 - d)
        acc_ref[...] = c[1]

    o_ref[...] = acc_ref[...].astype(o_ref.dtype)


def sb_prompt(proj, k_buf, v_buf, bias, *, layer, batch, seq, heads, head_dim, tq, tk, pieces):
    assert head_dim == LANES
    nq = seq // tq
    kernel = functools.partial(_sb_prompt_kernel, tq=tq, tk=tk, scale=head_dim ** -0.5, pieces=pieces)
    kv_spec = pl.BlockSpec((None, seq, head_dim), lambda b, h, i: (layer, b, h))
    return pl.pallas_call(
        kernel,
        grid=(batch, heads, nq),
        in_specs=[
            pl.BlockSpec(memory_space=pltpu.SMEM),
            pl.BlockSpec((tq, head_dim), lambda b, h, i: (b * nq + i, h)),
            kv_spec, kv_spec,
        ],
        out_specs=pl.BlockSpec((tq, head_dim), lambda b, h, i: (b * nq + i, h)),
        out_shape=jax.ShapeDtypeStruct((batch * seq, heads * head_dim), BF16),
        scratch_shapes=[
            pltpu.VMEM((seq, head_dim), BF16), pltpu.VMEM((seq, head_dim), BF16),
            pltpu.VMEM((2, tq, tk), F32), pltpu.VMEM((2, tq, tk), F32), pltpu.VMEM((2, tq, pieces * tk), BF16),
            pltpu.VMEM((2, tq, tk), F32), pltpu.VMEM((2, tq, tk), BF16), pltpu.VMEM((tq, head_dim), F32),
        ],
        compiler_params=_params("parallel", "parallel", "arbitrary"),
        name="sb_prompt",
    )(bias, proj, k_buf, v_buf)


def _hgrn_gates(qb, fpre, lb, key_dim):
    e = jnp.exp(-jnp.abs(fpre))
    r = 1.0 / (1.0 + e)
    log_sig = jnp.minimum(fpre, 0.0) + jnp.log(r)
    sig_neg = jnp.where(fpre >= 0.0, e * r, r)
    a = jnp.log(lb)
    b = jnp.log(1.0 - lb) + log_sig
    logf = jnp.maximum(a, b) + jnp.log(1.0 + jnp.exp(-jnp.abs(a - b)))
    kh = (1.0 - lb) * sig_neg
    qh = qb * _sigmoid(qb) * (key_dim ** -0.5)
    return qh, kh, logf


def _hgrn_out(o, norm_w, gate):
    return _rms_scale(o, norm_w) * (gate * _sigmoid(gate))


def _level_mask_table(chunk):
    t = np.arange(chunk)[:, None]
    s = np.arange(chunk)[None, :]
    table = np.full((chunk, chunk), -1, np.int32)
    m, lvl = SUBLANES, 0
    while m < chunk:
        sibling = (t // (2 * m) == s // (2 * m)) & (t % (2 * m) >= m) & (s % (2 * m) < m)
        table[sibling] = lvl
        m, lvl = 2 * m, lvl + 1
    return table


def _hgrn_prompt_kernel(q_ref, f_ref, i_ref, g_ref, lb_ref, nw_ref, lm_ref, o_ref, s_ref, state_ref, *, chunk, hps):
    hb = pl.program_id(1)
    ci = pl.program_id(2)
    kd = q_ref.shape[-1] // hps

    @pl.when(ci == 0)
    def _():
        state_ref[...] = jnp.zeros_like(state_ref)

    row = lax.broadcasted_iota(jnp.int32, (chunk, chunk), 0)
    col = lax.broadcasted_iota(jnp.int32, (chunk, chunk), 1)
    prefix_incl = jnp.where(col <= row, 1.0, 0.0).astype(BF16)
    rows = lax.broadcasted_iota(jnp.int32, (chunk, kd), 0)
    lm = lm_ref[...]
    groups = chunk // SUBLANES
    sub = lax.broadcasted_iota(jnp.int32, (groups, SUBLANES, kd), 1)

    for hh in range(hps):
        lanes = slice(hh * kd, (hh + 1) * kd)
        lb = lb_ref[pl.ds(hb * hps + hh, 1), :]
        qh, kh, logf = _hgrn_gates(q_ref[:, lanes], f_ref[:, lanes], lb, kd)
        vh = i_ref[:, lanes]
        vh_b = vh.astype(BF16)

        b = sum(_dot(prefix_incl, part) for part in _split_bf16(logf, 3))

        state = state_ref[hh]
        o = _dot((qh * jnp.exp(b)).astype(BF16), state.astype(BF16))

        scores = jnp.zeros((chunk, chunk), F32)
        m, lvl = SUBLANES, 0
        while m < chunk:
            bnd = jnp.concatenate(
                [jnp.broadcast_to(b[p + m - 1:p + m, :], (2 * m, kd)) for p in range(0, chunk, 2 * m)], axis=0)
            second = (rows & (2 * m - 1)) >= m
            d = b - bnd
            qt = jnp.where(second, qh * jnp.exp(d), 0.0)
            kt = jnp.where(second, 0.0, kh * jnp.exp(-d))
            scores = jnp.where(lm == lvl, _dot_nt(qt.astype(BF16), kt.astype(BF16)), scores)
            m, lvl = 2 * m, lvl + 1
        o = o + _dot(scores.astype(BF16), vh_b)

        q3 = qh.reshape(groups, SUBLANES, kd)
        k3 = kh.reshape(groups, SUBLANES, kd)
        b3 = b.reshape(groups, SUBLANES, kd)
        v3 = vh.reshape(groups, SUBLANES, kd)
        o3 = jnp.zeros((groups, SUBLANES, kd), F32)
        for s in range(SUBLANES):
            decay = jnp.exp(b3 - b3[:, s:s + 1, :])
            a = jnp.sum(q3 * k3[:, s:s + 1, :] * decay, axis=-1, keepdims=True)
            o3 = o3 + jnp.where(sub >= s, a, 0.0) * v3[:, s:s + 1, :]
        o = o + o3.reshape(chunk, kd)

        b_last = b[chunk - 1:chunk, :]
        k_dec = kh * jnp.exp(b_last - b)
        decay_col = jnp.broadcast_to(jnp.exp(b_last), (kd, kd)).T
        state_ref[hh] = decay_col * state + _dot(k_dec.T.astype(BF16), vh_b)

        o_ref[:, lanes] = _hgrn_out(o, nw_ref[pl.ds(hb * hps + hh, 1), :], g_ref[:, lanes]).astype(o_ref.dtype)

    @pl.when(ci == pl.num_programs(2) - 1)
    def _():
        s_ref[0] = state_ref[...]


def hgrn_prompt(proj, lb, norm_w, *, batch, seq, heads, key_dim, col0, chunk, hps):
    assert key_dim == LANES and heads % hps == 0 and col0 % hps == 0
    nc = seq // chunk
    lm = jnp.asarray(_level_mask_table(chunk))

    def seg(k):
        return pl.BlockSpec((chunk, hps * key_dim), lambda b, h, c: (b * nc + c, (col0 + k * heads) // hps + h))

    whole = lambda shape: pl.BlockSpec(shape, lambda b, h, c: (0,) * len(shape))
    return pl.pallas_call(
        functools.partial(_hgrn_prompt_kernel, chunk=chunk, hps=hps),
        grid=(batch, heads // hps, nc),
        in_specs=[seg(0), seg(1), seg(2), seg(3), whole((heads, key_dim)), whole((heads, key_dim)),
                  whole((chunk, chunk))],
        out_specs=[
            pl.BlockSpec((chunk, hps * key_dim), lambda b, h, c: (b * nc + c, h)),
            pl.BlockSpec((1, hps, key_dim, key_dim), lambda b, h, c: (b, h, 0, 0)),
        ],
        out_shape=[
            jax.ShapeDtypeStruct((batch * seq, heads * key_dim), BF16),
            jax.ShapeDtypeStruct((batch, heads, key_dim, key_dim), F32),
        ],
        scratch_shapes=[pltpu.VMEM((hps, key_dim, key_dim), F32)],
        compiler_params=_params("parallel", "parallel", "arbitrary"),
        name="hgrn_prompt",
    )(proj, proj, proj, proj, lb, norm_w, lm)


def _merge_out_kernel(*refs, gate_blocks):
    n_in = 3 + 2 * gate_blocks
    prompt, sample = refs[:n_in], refs[n_in:2 * n_in]
    wa_ref, wb_ref, wo_ref, g_ref, o_ref, os_ref = refs[2 * n_in:]

    rows = lambda k, dtype=F32: jnp.concatenate([prompt[k][...].astype(dtype), sample[k][...].astype(dtype)], axis=0)
    gate = lambda ks: jnp.concatenate([_sigmoid(rows(k)) for k in ks], axis=1)
    a = _dot(rows(0, BF16), wa_ref[...])
    b = _dot(rows(1, BF16), wb_ref[...])
    merged = gate(range(2, 2 + gate_blocks)) * a + gate(range(2 + gate_blocks, 2 + 2 * gate_blocks)) * b
    y = _dot(merged.astype(BF16), wo_ref[...])
    out = rows(n_in - 1) + _rms_scale(y, g_ref[...])
    tm = o_ref.shape[0]
    o_ref[...] = out[:tm]
    os_ref[...] = out[tm:]


def merge_out(prompt, sample, wa_all, wb_all, wo_all, gain, *, layer, tm, gate_off):
    m, d = prompt[3].shape
    rows_s = sample[3].shape[0]
    wa_rows, wb_rows = wa_all.shape[1], wb_all.shape[1]
    gw = int(np.gcd(gate_off, d))
    gate_blocks = d // gw
    const = lambda *shape: pl.BlockSpec((None, *shape), lambda i: (layer, 0, 0), pipeline_mode=pl.Buffered(1))

    def group(rows, row_index):
        at = lambda c: (lambda i: (row_index(i), c))
        return ([pl.BlockSpec((rows, wa_rows), at(0)), pl.BlockSpec((rows, wb_rows), at(0))]
                + [pl.BlockSpec((rows, gw), at(gate_off // gw + c)) for c in range(2 * gate_blocks)]
                + [pl.BlockSpec((rows, d), at(0))])

    operands = lambda t: (t[0], t[1], *([t[2]] * (2 * gate_blocks)), t[3])
    return pl.pallas_call(
        functools.partial(_merge_out_kernel, gate_blocks=gate_blocks),
        grid=(m // tm,),
        in_specs=group(tm, lambda i: i) + group(rows_s, lambda i: 0)
        + [const(wa_rows, d), const(wb_rows, d), const(d, d), pl.BlockSpec((1, d), lambda i: (0, 0))],
        out_specs=[pl.BlockSpec((tm, d), lambda i: (i, 0)), pl.BlockSpec((rows_s, d), lambda i: (0, 0))],
        out_shape=[jax.ShapeDtypeStruct((m, d), F32), jax.ShapeDtypeStruct((rows_s, d), F32)],
        compiler_params=_params("arbitrary"),
        name="merge_out_proj",
    )(*operands(prompt), *operands(sample), wa_all, wb_all, wo_all, gain.reshape(1, d))


def _mlp_kernel(x_ref, xs_ref, gpre_ref, wu_ref, wd_ref, gpost_ref, *rest, feeds_next):
    if feeds_next:
        gnext_ref, o_ref, os_ref, xn_ref, xns_ref, hn_ref, acc_ref = rest
    else:
        o_ref, os_ref, hn_ref, acc_ref = rest
    j = pl.program_id(1)
    tm = x_ref.shape[0]

    @pl.when(j == 0)
    def _():
        hn_ref[:tm, :] = _rms_scale(x_ref[...], gpre_ref[...]).astype(BF16)
        hn_ref[tm:, :] = _rms_scale(xs_ref[...], gpre_ref[...]).astype(BF16)
        acc_ref[...] = jnp.zeros_like(acc_ref)

    u = jnp.maximum(_dot(hn_ref[...], wu_ref[...]), 0.0)
    acc_ref[...] += _dot((u * u).astype(BF16), wd_ref[...])

    @pl.when(j == pl.num_programs(1) - 1)
    def _():
        out = x_ref[...] + _rms_scale(acc_ref[:tm, :], gpost_ref[...])
        outs = xs_ref[...] + _rms_scale(acc_ref[tm:, :], gpost_ref[...])
        o_ref[...] = out
        os_ref[...] = outs
        if feeds_next:
            xn_ref[...] = _rms_scale(out, gnext_ref[...]).astype(BF16)
            xns_ref[...] = _rms_scale(outs, gnext_ref[...]).astype(BF16)


def mlp(x, xs, gpre, wu_all, wd_all, gpost, gnext, *, layer, tm, tf):
    m, d = x.shape
    rows_s = xs.shape[0]
    f = wu_all.shape[2]
    vec = pl.BlockSpec((1, d), lambda i, j: (0, 0))
    rows = pl.BlockSpec((tm, d), lambda i, j: (i, 0))
    sample = pl.BlockSpec((rows_s, d), lambda i, j: (0, 0))
    feeds_next = gnext is not None
    out_specs = [rows, sample] + ([rows, sample] if feeds_next else [])
    out_shape = [jax.ShapeDtypeStruct((m, d), F32), jax.ShapeDtypeStruct((rows_s, d), F32)]
    if feeds_next:
        out_shape += [jax.ShapeDtypeStruct((m, d), BF16), jax.ShapeDtypeStruct((rows_s, d), BF16)]
    return pl.pallas_call(
        functools.partial(_mlp_kernel, feeds_next=feeds_next),
        grid=(m // tm, f // tf),
        in_specs=[
            rows, sample, vec,
            pl.BlockSpec((None, d, tf), lambda i, j: (layer, 0, j)),
            pl.BlockSpec((None, tf, d), lambda i, j: (layer, j, 0)),
            vec,
        ] + ([vec] if feeds_next else []),
        out_specs=out_specs,
        out_shape=out_shape,
        scratch_shapes=[pltpu.VMEM((tm + rows_s, d), BF16), pltpu.VMEM((tm + rows_s, d), F32)],
        compiler_params=_params("arbitrary", "arbitrary"),
        name="mlp",
    )(x, xs, gpre.reshape(1, d), wu_all, wd_all, gpost.reshape(1, d), *([gnext.reshape(1, d)] if feeds_next else []))


def _sb_sample_kernel(pt_ref, q_ref, bias_ref, kn_ref, vn_ref, *rest, pages_per_step, heads, page, past, scale):
    del pt_ref
    k_refs = rest[:pages_per_step]
    v_refs = rest[pages_per_step:2 * pages_per_step]
    o_ref, later_ref, acc_ref = rest[2 * pages_per_step:]
    g = pl.program_id(1)
    hp, hd = q_ref.shape[1], q_ref.shape[2]
    width = page * heads
    nblk = width // LANES

    qs = q_ref[0] * scale
    q = qs.astype(BF16)

    @pl.when(g == 0)
    def _():
        z = jnp.sum(qs * kn_ref[0], axis=-1, keepdims=True) + bias_ref[:, 0:1]
        k_pos = past + lax.broadcasted_iota(jnp.int32, (hp, 1), 1)
        mask = k_pos < past
        sp = _softplus(z)
        later_ref[...] = jnp.broadcast_to(jnp.where(mask, -sp, 0.0), (hp, LANES))
        acc_ref[...] = jnp.where(mask, jnp.exp(z - sp), 0.0) * vn_ref[0]

    head_row = lax.broadcasted_iota(jnp.int32, (hp, width), 0)
    head_lane = lax.broadcasted_iota(jnp.int32, (hp, width), 1) % heads
    own = head_row == head_lane
    r = lax.broadcasted_iota(jnp.int32, (LANES, 2 * LANES), 0)
    c = lax.broadcasted_iota(jnp.int32, (LANES, 2 * LANES), 1)
    neg_sum = jnp.where((c >= LANES) | ((r % heads == c % heads) & (r // heads >= c // heads)), -1.0, 0.0).astype(BF16)
    bias = bias_ref[...]

    order = list(reversed(range(pages_per_step)))
    zs = [_dot_nt(q, k_refs[i][0, 0].reshape(width, hd).astype(BF16)) + bias for i in order]
    sps = [jnp.where(own, _softplus(z), 0.0) for z in zs]
    stacked = jnp.concatenate(
        [sp[:, b * LANES:(b + 1) * LANES] for sp in sps for b in reversed(range(nblk))], axis=0)
    s_hi, s_lo = _split_bf16(stacked, 2)
    sums = _dot(s_hi, neg_sum) + _dot(s_lo, neg_sum)

    later = later_ref[...]
    acc = acc_ref[...]
    for n, i in enumerate(order):
        pieces = [None] * nblk
        for m, b in enumerate(reversed(range(nblk))):
            rows = slice((n * nblk + m) * hp, (n * nblk + m + 1) * hp)
            blk = slice(b * LANES, (b + 1) * LANES)
            pieces[b] = jnp.where(own[:, blk], jnp.exp(zs[n][:, blk] + sums[rows, :LANES] + later), 0.0)
            later = later + sums[rows, LANES:]
        w = jnp.concatenate(pieces, axis=1).astype(BF16)
        acc = acc + _dot(w, v_refs[i][0, 0].reshape(width, hd).astype(BF16))
    later_ref[...] = later
    acc_ref[...] = acc

    @pl.when(g == pl.num_programs(1) - 1)
    def _():
        o_ref[0] = acc


def sb_sample(q, bias_b, k_new, v_new, cache_k, cache_v, page_table, *, layer, pages_per_step):
    nb, hp, hd = q.shape
    _, _, page, heads, _ = cache_k.shape
    n_pages = page_table.shape[1]
    groups = n_pages // pages_per_step
    assert LANES % heads == 0 and hd == LANES

    def page_spec(i):
        def index(b, g, pt):
            return (layer, pt[b, (groups - 1 - g) * pages_per_step + i], 0, 0, 0)
        return pl.BlockSpec((1, 1, page, heads, hd), index)

    per_sample = pl.BlockSpec((1, hp, hd), lambda b, g, pt: (b, 0, 0))
    kernel = functools.partial(_sb_sample_kernel, pages_per_step=pages_per_step, heads=heads, page=page,
                               past=n_pages * page, scale=hd ** -0.5)
    grid_spec = pltpu.PrefetchScalarGridSpec(
        num_scalar_prefetch=1,
        grid=(nb, groups),
        in_specs=[per_sample, pl.BlockSpec((hp, page * heads), lambda b, g, pt: (0, 0)), per_sample, per_sample]
        + [page_spec(i) for i in range(pages_per_step)] * 2,
        out_specs=pl.BlockSpec((1, hp, hd), lambda b, g, pt: (b, 0, 0)),
        scratch_shapes=[pltpu.VMEM((hp, LANES), F32), pltpu.VMEM((hp, hd), F32)],
    )
    return pl.pallas_call(
        kernel,
        grid_spec=grid_spec,
        out_shape=jax.ShapeDtypeStruct((nb, hp, hd), F32),
        compiler_params=_params("parallel", "arbitrary"),
        name="sb_sample",
    )(page_table, q, bias_b, k_new, v_new, *([cache_k] * pages_per_step), *([cache_v] * pages_per_step))


def _hgrn_sample_kernel(q_ref, f_ref, i_ref, g_ref, lb_ref, nw_ref, s_ref, o_ref, so_ref, *, samples):
    h = pl.program_id(0)
    kd = q_ref.shape[-1]
    lb = lb_ref[pl.ds(h, 1), :]
    qh, kh, logf = _hgrn_gates(q_ref[...], f_ref[...], lb, kd)
    vh = i_ref[...]
    f = jnp.exp(logf)
    qk = jnp.sum(qh * kh, axis=-1, keepdims=True)
    o_ref[...] = jnp.zeros_like(o_ref)
    for b in range(samples):
        row = slice(b, b + 1)
        state = s_ref[b, 0]
        f_col = jnp.broadcast_to(f[row], (kd, kd)).T
        k_col = jnp.broadcast_to(kh[row], (kd, kd)).T
        so_ref[b, 0] = f_col * state + k_col * vh[row]
        q_dec = jnp.broadcast_to(qh[row] * f[row], (SUBLANES, kd)).astype(BF16)
        o = _dot(q_dec, state.astype(BF16))[0:1] + qk[row] * vh[row]
        o_ref[row, :] = _hgrn_out(o, nw_ref[pl.ds(h, 1), :], g_ref[row, :])


def hgrn_sample(proj, lb, norm_w, state, *, layer, heads, key_dim, col0, samples):
    rows = proj.shape[0]

    def seg(k):
        return pl.BlockSpec((rows, key_dim), lambda h: (0, col0 + k * heads + h))

    whole = pl.BlockSpec((heads, key_dim), lambda h: (0, 0))
    return pl.pallas_call(
        functools.partial(_hgrn_sample_kernel, samples=samples),
        grid=(heads,),
        in_specs=[seg(0), seg(1), seg(2), seg(3), whole, whole,
                  pl.BlockSpec((None, samples, 1, key_dim, key_dim), lambda h: (layer, 0, h, 0, 0))],
        out_specs=[
            pl.BlockSpec((rows, key_dim), lambda h: (0, h)),
            pl.BlockSpec((samples, 1, key_dim, key_dim), lambda h: (0, h, 0, 0)),
        ],
        out_shape=[
            jax.ShapeDtypeStruct((rows, heads * key_dim), F32),
            jax.ShapeDtypeStruct((samples, heads, key_dim, key_dim), F32),
        ],
        compiler_params=_params("parallel"),
        name="hgrn_sample",
    )(proj, proj, proj, proj, lb, norm_w, state)


SAMPLE_ROWS = 16
TILES = dict(norm_tm=1024, in_tm=2048, in_tn=512, sb_tq=512, sb_tk=256, sb_pieces=1, hg_chunk=256, hg_heads=4, merge_tm=256, mlp_tm=512,
             mlp_tf=512, sample_pages=8)


def kernel(x_prompt, x_sample, cache_k, cache_v, state_hgrn, page_table, norm_mix_pre, norm_mix_post,
           norm_mlp_pre, norm_mlp_post, w_in, sb_bias, lower_bounds, hgrn_norm, w_branch_a, w_branch_b, w_out,
           w_up, w_down):
    batch, seq, d_model = x_prompt.shape
    nb, dec_seq, _ = x_sample.shape
    depth, _, page, heads, head_dim = cache_k.shape
    assert dec_seq == 1
    sb_width = heads * head_dim
    key_width = lower_bounds.shape[1]
    key_dim = key_width // heads
    hg_col0 = 3 * sb_width // key_dim
    gate_off = 3 * sb_width + 4 * key_width
    t = TILES

    lb_all = jnp.cumsum(jax.nn.softmax(lower_bounds.astype(F32), axis=0), axis=0)
    lb_all = jnp.maximum(lb_all - lb_all[:1], 0.0).reshape(depth, heads, key_dim)
    norm_w = hgrn_norm.reshape(depth, heads, key_dim)
    w_a_b, w_b_b, w_o_b, w_up_b, w_down_b = (w.astype(BF16) for w in (w_branch_a, w_branch_b, w_out, w_up, w_down))

    xp = x_prompt.reshape(batch * seq, d_model)
    xs = jnp.pad(x_sample.reshape(nb, d_model), ((0, SAMPLE_ROWS - nb), (0, 0)))
    pad_heads = lambda a: jnp.pad(a.reshape(nb, heads, head_dim), ((0, 0), (0, SAMPLE_ROWS - heads), (0, 0)))
    s_prompt, k_sample, v_sample, s_sample = [], [], [], []
    k_buf = v_buf = None
    xn, xns = norm_cast(xp, xs, norm_mix_pre[0], tm=t["norm_tm"])
    for l in range(depth):
        proj, proj_s, k_buf, v_buf = in_proj(xn, xns, w_in, layer=l, tm=t["in_tm"], tn=t["in_tn"], col0=sb_width,
                                             width=sb_width, k_buf=k_buf, v_buf=v_buf)
        o_a = sb_prompt(proj, k_buf, v_buf, sb_bias[l], layer=l, batch=batch, seq=seq, heads=heads,
                        head_dim=head_dim, tq=t["sb_tq"], tk=t["sb_tk"], pieces=t["sb_pieces"])
        o_b, s_p = hgrn_prompt(proj, lb_all[l], norm_w[l], batch=batch, seq=seq, heads=heads, key_dim=key_dim,
                               col0=hg_col0, chunk=t["hg_chunk"], hps=t["hg_heads"])
        k_s = proj_s[:nb, sb_width:2 * sb_width]
        v_s = proj_s[:nb, 2 * sb_width:3 * sb_width]
        bias_b = jnp.pad(jnp.broadcast_to(sb_bias[l][:, None], (heads, page * heads)),
                         ((0, SAMPLE_ROWS - heads), (0, 0)))
        o_as = sb_sample(pad_heads(proj_s[:nb, :sb_width]), bias_b, pad_heads(k_s), pad_heads(v_s), cache_k, cache_v,
                         page_table, layer=l, pages_per_step=t["sample_pages"])
        o_as = jnp.pad(o_as[:, :heads].reshape(nb, sb_width), ((0, SAMPLE_ROWS - nb), (0, 0)))
        o_bs, s_s = hgrn_sample(proj_s, lb_all[l], norm_w[l], state_hgrn, layer=l, heads=heads, key_dim=key_dim,
                                col0=hg_col0, samples=nb)
        xp, xs = merge_out((o_a, o_b, proj, xp), (o_as, o_bs, proj_s, xs), w_a_b, w_b_b, w_o_b, norm_mix_post[l],
                           layer=l, tm=t["merge_tm"], gate_off=gate_off)
        xp, xs, *normed = mlp(xp, xs, norm_mlp_pre[l], w_up_b, w_down_b, norm_mlp_post[l],
                              norm_mix_pre[l + 1] if l + 1 < depth else None, layer=l, tm=t["mlp_tm"], tf=t["mlp_tf"])
        if normed:
            xn, xns = normed
        s_prompt.append(s_p)
        k_sample.append(k_s.reshape(nb, dec_seq, heads, head_dim))
        v_sample.append(v_s.reshape(nb, dec_seq, heads, head_dim))
        s_sample.append(s_s)

    k_p, v_p = (buf.reshape(depth, batch, seq, heads, head_dim) for buf in (k_buf, v_buf))
    return (xp.reshape(batch, seq, d_model), xs[:nb].reshape(nb, dec_seq, d_model), k_p, v_p, jnp.stack(s_prompt),
            jnp.stack(k_sample), jnp.stack(v_sample), jnp.stack(s_sample))
```

```python
import functools

import numpy as np
import jax
import jax.numpy as jnp
from jax import lax
from jax.experimental import pallas as pl
from jax.experimental.pallas import tpu as pltpu

F32 = jnp.float32
BF16 = jnp.bfloat16
RMS_EPS = 1e-6
LANES = 128
SUBLANES = 8
VMEM_LIMIT = 56 * 1024 * 1024

NT_DIMS = (((1,), (1,)), ((), ()))


def _dot(a, b):
    return jnp.dot(a, b, preferred_element_type=F32)


def _dot_nt(a, b):
    return lax.dot_general(a, b, NT_DIMS, preferred_element_type=F32)


def _sigmoid(x):
    return 1.0 / (1.0 + jnp.exp(-x))


def _softplus(x):
    return jnp.maximum(x, 0.0) + jnp.log(1.0 + jnp.exp(-jnp.abs(x)))


def _rms_scale(x, w):
    return x * lax.rsqrt(jnp.mean(x * x, axis=-1, keepdims=True) + RMS_EPS) * w


def _split_bf16(x, parts):
    out = []
    for _ in range(parts - 1):
        hi = x.astype(BF16)
        out.append(hi)
        x = x - hi.astype(F32)
    out.append(x.astype(BF16))
    return out


def _params(*sem):
    return pltpu.CompilerParams(dimension_semantics=sem, vmem_limit_bytes=VMEM_LIMIT)


def _norm_cast_kernel(x_ref, xs_ref, g_ref, o_ref, os_ref):
    o_ref[...] = _rms_scale(x_ref[...], g_ref[...]).astype(BF16)
    os_ref[...] = _rms_scale(xs_ref[...], g_ref[...]).astype(BF16)


def norm_cast(x, xs, gain, *, tm):
    m, d = x.shape
    rows_s = xs.shape[0]
    sample = pl.BlockSpec((rows_s, d), lambda i: (0, 0))
    return pl.pallas_call(
        _norm_cast_kernel,
        grid=(m // tm,),
        in_specs=[pl.BlockSpec((tm, d), lambda i: (i, 0)), sample, pl.BlockSpec((1, d), lambda i: (0, 0))],
        out_specs=[pl.BlockSpec((tm, d), lambda i: (i, 0)), sample],
        out_shape=[jax.ShapeDtypeStruct((m, d), BF16), jax.ShapeDtypeStruct((rows_s, d), BF16)],
        compiler_params=_params("arbitrary"),
        name="norm_cast",
    )(x, xs, gain.reshape(1, d))


def _in_proj_kernel(x_ref, xs_ref, w_ref, *rest, kv_tiles):
    o_ref, os_ref, k_ref, v_ref = rest[-4:]
    i = pl.program_id(0)
    j = pl.program_id(1)
    w = w_ref[...].astype(BF16)
    res = _dot(x_ref[...], w)
    o_ref[...] = res

    @pl.when(i == 0)
    def _():
        os_ref[...] = _dot(xs_ref[...], w)

    k0, v0, n = kv_tiles

    @pl.when((j >= k0) & (j < k0 + n))
    def _():
        k_ref[...] = res

    @pl.when((j >= v0) & (j < v0 + n))
    def _():
        v_ref[...] = res


def in_proj(x, xs, w_all, *, layer, tm, tn, col0, width, k_buf, v_buf):
    m, d = x.shape
    rows_s = xs.shape[0]
    depth, _, n = w_all.shape
    nj = n // tn
    in_specs = [
        pl.BlockSpec((tm, d), lambda i, j: (i, 0)),
        pl.BlockSpec((rows_s, d), lambda i, j: (0, 0)),
        pl.BlockSpec((None, d, tn), lambda i, j: (layer, 0, j)),
    ]
    out_specs = [pl.BlockSpec((tm, tn), lambda i, j: (i, j)),
                 pl.BlockSpec((rows_s, tn), lambda i, j: (0, jnp.where(i == 0, j, nj - 1)))]
    out_shape = [jax.ShapeDtypeStruct((m, n), F32), jax.ShapeDtypeStruct((rows_s, n), F32)]
    args = [x, xs, w_all]
    nt = width // tn
    kv_tiles = (col0 // tn, (col0 + width) // tn, nt)
    for first in kv_tiles[:2]:
        index = functools.partial(lambda i, j, f: (layer, i, jnp.clip(j - f, 0, nt - 1)), f=first)
        out_specs.append(pl.BlockSpec((None, tm, tn), index))
        out_shape.append(jax.ShapeDtypeStruct((depth, m, width), F32))
    aliases = {}
    if k_buf is not None:
        in_specs += [pl.BlockSpec(memory_space=pl.ANY)] * 2
        args += [k_buf, v_buf]
        aliases = {3: 2, 4: 3}
    return pl.pallas_call(
        functools.partial(_in_proj_kernel, kv_tiles=kv_tiles),
        grid=(m // tm, nj),
        in_specs=in_specs,
        out_specs=out_specs,
        out_shape=out_shape,
        input_output_aliases=aliases,
        compiler_params=_params("arbitrary", "arbitrary"),
        name="in_proj",
    )(*args)


def _sb_prompt_kernel(bias_ref, q_ref, k_ref, v_ref, o_ref, kb_ref, vb_ref, zr_ref, z_ref, s_ref, in_ref, w_ref,
                      acc_ref, *, tq, tk, scale, pieces):
    h = pl.program_id(1)
    i = pl.program_id(2)
    n_sub = tq // tk
    assert n_sub == 2

    @pl.when(i == 0)
    def _():
        kb_ref[...] = k_ref[...].astype(BF16)
        vb_ref[...] = v_ref[...].astype(BF16)

    q = (q_ref[...] * scale).astype(BF16)
    bias = bias_ref[h]
    row = lax.broadcasted_iota(jnp.int32, (pieces * tk, tk), 0) % tk
    col = lax.broadcasted_iota(jnp.int32, (pieces * tk, tk), 1)
    neg_suffix = jnp.where(row >= col, -1.0, 0.0).astype(BF16)
    local_q = lax.broadcasted_iota(jnp.int32, (tq, tk), 0)
    local_k = lax.broadcasted_iota(jnp.int32, (tq, tk), 1)

    nblk = (i + 1) * n_sub
    ks_of = lambda n: pl.multiple_of((nblk - 1 - n) * tk, tk)
    mask_of = lambda n: (local_k + (tk if n == 0 else 0)) < local_q

    def st_scores(slot, n):
        zr_ref[slot] = _dot_nt(q, kb_ref[pl.ds(ks_of(n), tk), :]) + bias

    def st_softplus(slot, later, n=None):
        z = zr_ref[slot]
        sp = _softplus(z)
        if n is not None:
            sp = jnp.where(mask_of(n), sp, 0.0)
        z_ref[slot] = z + later
        s_ref[slot] = jnp.concatenate(_split_bf16(sp, pieces), axis=1)
        return later - jnp.sum(sp, axis=-1, keepdims=True)

    def st_suffix(slot):
        in_ref[slot] = _dot(s_ref[slot], neg_suffix)

    def st_weights(slot, n=None):
        w = jnp.exp(z_ref[slot] + in_ref[slot])
        if n is not None:
            w = jnp.where(mask_of(n), w, 0.0)
        w_ref[slot] = w.astype(BF16)

    def st_values(slot, n, acc):
        return acc + _dot(w_ref[slot], vb_ref[pl.ds(ks_of(n), tk), :])

    zero_later = jnp.zeros((tq, 1), F32)
    zero_acc = jnp.zeros((tq, LANES), F32)

    @pl.when(i < 2)
    def _():
        def block(n, later, acc, masked):
            st_scores(0, n)
            later = st_softplus(0, later, n if masked else None)
            st_suffix(0)
            st_weights(0, n if masked else None)
            return later, st_values(0, n, acc)

        carry = block(1, *block(0, zero_later, zero_acc, True), True)
        _, acc = lax.fori_loop(n_sub, nblk, lambda n, c: block(n, *c, False), carry)
        acc_ref[...] = acc

    @pl.when(i >= 2)
    def _():
        def trip(n, p, later, acc, first=-1, last=4):
            static = isinstance(n, int)
            if first < 0 <= last:
                acc = st_values(p, n, acc)
            if first < 1 <= last:
                st_weights(1 - p, n + 1 if static and n + 1 < n_sub else None)
            if first < 2 <= last:
                st_suffix(p)
            if first < 3 <= last:
                later = st_softplus(1 - p, later, n + 3 if static and n + 3 < n_sub else None)
            if first < 4 <= last:
                st_scores(p, n + 4)
            return later, acc

        c = (zero_later, zero_acc)
        for n in range(-4, 2):
            c = trip(n, n % 2, *c, first=-n - 1)

        def two_trips(pair, c):
            n = 2 * pair + 2
            return trip(n + 1, 1, *trip(n, 0, *c))

        c = lax.fori_loop(0, i - 2, two_trips, c)
        for d in range(4):
            c = trip(nblk - 4 + d, d % 2, *c, last=3 - d)
        acc_ref[...] = c[1]

    o_ref[...] = acc_ref[...].astype(o_ref.dtype)


def sb_prompt(proj, k_buf, v_buf, bias, *, layer, batch, seq, heads, head_dim, tq, tk, pieces):
    assert head_dim == LANES
    nq = seq // tq
    kernel = functools.partial(_sb_prompt_kernel, tq=tq, tk=tk, scale=head_dim ** -0.5, pieces=pieces)
    kv_spec = pl.BlockSpec((None, seq, head_dim), lambda b, h, i: (layer, b, h))
    return pl.pallas_call(
        kernel,
        grid=(batch, heads, nq),
        in_specs=[
            pl.BlockSpec(memory_space=pltpu.SMEM),
            pl.BlockSpec((tq, head_dim), lambda b, h, i: (b * nq + i, h)),
            kv_spec, kv_spec,
        ],
        out_specs=pl.BlockSpec((tq, head_dim), lambda b, h, i: (b * nq + i, h)),
        out_shape=jax.ShapeDtypeStruct((batch * seq, heads * head_dim), BF16),
        scratch_shapes=[
            pltpu.VMEM((seq, head_dim), BF16), pltpu.VMEM((seq, head_dim), BF16),
            pltpu.VMEM((2, tq, tk), F32), pltpu.VMEM((2, tq, tk), F32), pltpu.VMEM((2, tq, pieces * tk), BF16),
            pltpu.VMEM((2, tq, tk), F32), pltpu.VMEM((2, tq, tk), BF16), pltpu.VMEM((tq, head_dim), F32),
        ],
        compiler_params=_params("parallel", "parallel", "arbitrary"),
        name="sb_prompt",
    )(bias, proj, k_buf, v_buf)


def _hgrn_gates(qb, fpre, lb, key_dim):
    e = jnp.exp(-jnp.abs(fpre))
    r = 1.0 / (1.0 + e)
    log_sig = jnp.minimum(fpre, 0.0) + jnp.log(r)
    sig_neg = jnp.where(fpre >= 0.0, e * r, r)
    a = jnp.log(lb)
    b = jnp.log(1.0 - lb) + log_sig
    logf = jnp.maximum(a, b) + jnp.log(1.0 + jnp.exp(-jnp.abs(a - b)))
    kh = (1.0 - lb) * sig_neg
    qh = qb * _sigmoid(qb) * (key_dim ** -0.5)
    return qh, kh, logf


def _hgrn_out(o, norm_w, gate):
    return _rms_scale(o, norm_w) * (gate * _sigmoid(gate))


def _level_mask_table(chunk):
    t = np.arange(chunk)[:, None]
    s = np.arange(chunk)[None, :]
    table = np.full((chunk, chunk), -1, np.int32)
    m, lvl = SUBLANES, 0
    while m < chunk:
        sibling = (t // (2 * m) == s // (2 * m)) & (t % (2 * m) >= m) & (s % (2 * m) < m)
        table[sibling] = lvl
        m, lvl = 2 * m, lvl + 1
    return table


def _hgrn_prompt_kernel(q_ref, f_ref, i_ref, g_ref, lb_ref, nw_ref, lm_ref, o_ref, s_ref, state_ref, *, chunk, hps):
    hb = pl.program_id(1)
    ci = pl.program_id(2)
    kd = q_ref.shape[-1] // hps

    @pl.when(ci == 0)
    def _():
        state_ref[...] = jnp.zeros_like(state_ref)

    row = lax.broadcasted_iota(jnp.int32, (chunk, chunk), 0)
    col = lax.broadcasted_iota(jnp.int32, (chunk, chunk), 1)
    prefix_incl = jnp.where(col <= row, 1.0, 0.0).astype(BF16)
    rows = lax.broadcasted_iota(jnp.int32, (chunk, kd), 0)
    lm = lm_ref[...]
    groups = chunk // SUBLANES
    sub = lax.broadcasted_iota(jnp.int32, (groups, SUBLANES, kd), 1)

    for hh in range(hps):
        lanes = slice(hh * kd, (hh + 1) * kd)
        lb = lb_ref[pl.ds(hb * hps + hh, 1), :]
        qh, kh, logf = _hgrn_gates(q_ref[:, lanes], f_ref[:, lanes], lb, kd)
        vh = i_ref[:, lanes]
        vh_b = vh.astype(BF16)

        b = sum(_dot(prefix_incl, part) for part in _split_bf16(logf, 3))

        state = state_ref[hh]
        o = _dot((qh * jnp.exp(b)).astype(BF16), state.astype(BF16))

        scores = jnp.zeros((chunk, chunk), F32)
        m, lvl = SUBLANES, 0
        while m < chunk:
            bnd = jnp.concatenate(
                [jnp.broadcast_to(b[p + m - 1:p + m, :], (2 * m, kd)) for p in range(0, chunk, 2 * m)], axis=0)
            second = (rows & (2 * m - 1)) >= m
            d = b - bnd
            qt = jnp.where(second, qh * jnp.exp(d), 0.0)
            kt = jnp.where(second, 0.0, kh * jnp.exp(-d))
            scores = jnp.where(lm == lvl, _dot_nt(qt.astype(BF16), kt.astype(BF16)), scores)
            m, lvl = 2 * m, lvl + 1
        o = o + _dot(scores.astype(BF16), vh_b)

        q3 = qh.reshape(groups, SUBLANES, kd)
        k3 = kh.reshape(groups, SUBLANES, kd)
        b3 = b.reshape(groups, SUBLANES, kd)
        v3 = vh.reshape(groups, SUBLANES, kd)
        o3 = jnp.zeros((groups, SUBLANES, kd), F32)
        for s in range(SUBLANES):
            decay = jnp.exp(b3 - b3[:, s:s + 1, :])
            a = jnp.sum(q3 * k3[:, s:s + 1, :] * decay, axis=-1, keepdims=True)
            o3 = o3 + jnp.where(sub >= s, a, 0.0) * v3[:, s:s + 1, :]
        o = o + o3.reshape(chunk, kd)

        b_last = b[chunk - 1:chunk, :]
        k_dec = kh * jnp.exp(b_last - b)
        decay_col = jnp.broadcast_to(jnp.exp(b_last), (kd, kd)).T
        state_ref[hh] = decay_col * state + _dot(k_dec.T.astype(BF16), vh_b)

        o_ref[:, lanes] = _hgrn_out(o, nw_ref[pl.ds(hb * hps + hh, 1), :], g_ref[:, lanes]).astype(o_ref.dtype)

    @pl.when(ci == pl.num_programs(2) - 1)
    def _():
        s_ref[0] = state_ref[...]


def _merge_out_kernel(*refs, gate_blocks):
    n_in = 3 + 2 * gate_blocks
    prompt, sample = refs[:n_in], refs[n_in:2 * n_in]
    wa_ref, wb_ref, wo_ref, g_ref, o_ref, os_ref = refs[2 * n_in:]

    rows = lambda k, dtype=F32: jnp.concatenate([prompt[k][...].astype(dtype), sample[k][...].astype(dtype)], axis=0)
    gate = lambda ks: jnp.concatenate([_sigmoid(rows(k)) for k in ks], axis=1)
    a = _dot(rows(0, BF16), wa_ref[...])
    b = _dot(rows(1, BF16), wb_ref[...])
    merged = gate(range(2, 2 + gate_blocks)) * a + gate(range(2 + gate_blocks, 2 + 2 * gate_blocks)) * b
    y = _dot(merged.astype(BF16), wo_ref[...])
    out = rows(n_in - 1) + _rms_scale(y, g_ref[...])
    tm = o_ref.shape[0]
    o_ref[...] = out[:tm]
    os_ref[...] = out[tm:]


def merge_out(prompt, sample, wa_all, wb_all, wo_all, gain, *, layer, tm, gate_off):
    m, d = prompt[3].shape
    rows_s = sample[3].shape[0]
    wa_rows, wb_rows = wa_all.shape[1], wb_all.shape[1]
    gw = int(np.gcd(gate_off, d))
    gate_blocks = d // gw
    const = lambda *shape: pl.BlockSpec((None, *shape), lambda i: (layer, 0, 0), pipeline_mode=pl.Buffered(1))

    def group(rows, row_index):
        at = lambda c: (lambda i: (row_index(i), c))
        return ([pl.BlockSpec((rows, wa_rows), at(0)), pl.BlockSpec((rows, wb_rows), at(0))]
                + [pl.BlockSpec((rows, gw), at(gate_off // gw + c)) for c in range(2 * gate_blocks)]
                + [pl.BlockSpec((rows, d), at(0))])

    operands = lambda t: (t[0], t[1], *([t[2]] * (2 * gate_blocks)), t[3])
    return pl.pallas_call(
        functools.partial(_merge_out_kernel, gate_blocks=gate_blocks),
        grid=(m // tm,),
        in_specs=group(tm, lambda i: i) + group(rows_s, lambda i: 0)
        + [const(wa_rows, d), const(wb_rows, d), const(d, d), pl.BlockSpec((1, d), lambda i: (0, 0))],
        out_specs=[pl.BlockSpec((tm, d), lambda i: (i, 0)), pl.BlockSpec((rows_s, d), lambda i: (0, 0))],
        out_shape=[jax.ShapeDtypeStruct((m, d), F32), jax.ShapeDtypeStruct((rows_s, d), F32)],
        compiler_params=_params("arbitrary"),
        name="merge_out_proj",
    )(*operands(prompt), *operands(sample), wa_all, wb_all, wo_all, gain.reshape(1, d))


def _mlp_kernel(x_ref, xs_ref, gpre_ref, wu_ref, wd_ref, gpost_ref, *rest, feeds_next):
    if feeds_next:
        gnext_ref, o_ref, os_ref, xn_ref, xns_ref, hn_ref, acc_ref = rest
    else:
        o_ref, os_ref, hn_ref, acc_ref = rest
    j = pl.program_id(1)
    tm = x_ref.shape[0]

    @pl.when(j == 0)
    def _():
        hn_ref[:tm, :] = _rms_scale(x_ref[...], gpre_ref[...]).astype(BF16)
        hn_ref[tm:, :] = _rms_scale(xs_ref[...], gpre_ref[...]).astype(BF16)
        acc_ref[...] = jnp.zeros_like(acc_ref)

    u = jnp.maximum(_dot(hn_ref[...], wu_ref[...]), 0.0)
    acc_ref[...] += _dot((u * u).astype(BF16), wd_ref[...])

    @pl.when(j == pl.num_programs(1) - 1)
    def _():
        out = x_ref[...] + _rms_scale(acc_ref[:tm, :], gpost_ref[...])
        outs = xs_ref[...] + _rms_scale(acc_ref[tm:, :], gpost_ref[...])
        o_ref[...] = out
        os_ref[...] = outs
        if feeds_next:
            xn_ref[...] = _rms_scale(out, gnext_ref[...]).astype(BF16)
            xns_ref[...] = _rms_scale(outs, gnext_ref[...]).astype(BF16)


def mlp(x, xs, gpre, wu_all, wd_all, gpost, gnext, *, layer, tm, tf):
    m, d = x.shape
    rows_s = xs.shape[0]
    f = wu_all.shape[2]
    vec = pl.BlockSpec((1, d), lambda i, j: (0, 0))
    rows = pl.BlockSpec((tm, d), lambda i, j: (i, 0))
    sample = pl.BlockSpec((rows_s, d), lambda i, j: (0, 0))
    feeds_next = gnext is not None
    out_specs = [rows, sample] + ([rows, sample] if feeds_next else [])
    out_shape = [jax.ShapeDtypeStruct((m, d), F32), jax.ShapeDtypeStruct((rows_s, d), F32)]
    if feeds_next:
        out_shape += [jax.ShapeDtypeStruct((m, d), BF16), jax.ShapeDtypeStruct((rows_s, d), BF16)]
    return pl.pallas_call(
        functools.partial(_mlp_kernel, feeds_next=feeds_next),
        grid=(m // tm, f // tf),
        in_specs=[
            rows, sample, vec,
            pl.BlockSpec((None, d, tf), lambda i, j: (layer, 0, j)),
            pl.BlockSpec((None, tf, d), lambda i, j: (layer, j, 0)),
            vec,
        ] + ([vec] if feeds_next else []),
        out_specs=out_specs,
        out_shape=out_shape,
        scratch_shapes=[pltpu.VMEM((tm + rows_s, d), BF16), pltpu.VMEM((tm + rows_s, d), F32)],
        compiler_params=_params("arbitrary", "arbitrary"),
        name="mlp",
    )(x, xs, gpre.reshape(1, d), wu_all, wd_all, gpost.reshape(1, d), *([gnext.reshape(1, d)] if feeds_next else []))


def _sb_sample_kernel(pt_ref, q_ref, bias_ref, kn_ref, vn_ref, *rest, pages_per_step, heads, page, past, scale,
                      step_axis):
    del pt_ref
    k_refs = rest[:pages_per_step]
    v_refs = rest[pages_per_step:2 * pages_per_step]
    o_ref, later_ref, acc_ref = rest[2 * pages_per_step:]
    g = pl.program_id(step_axis)
    hp, hd = q_ref.shape[1], q_ref.shape[2]
    width = page * heads
    nblk = width // LANES

    qs = q_ref[0] * scale
    q = qs.astype(BF16)

    @pl.when(g == 0)
    def _():
        z = jnp.sum(qs * kn_ref[0], axis=-1, keepdims=True) + bias_ref[:, 0:1]
        k_pos = past + lax.broadcasted_iota(jnp.int32, (hp, 1), 1)
        mask = k_pos < past
        sp = _softplus(z)
        later_ref[...] = jnp.broadcast_to(jnp.where(mask, -sp, 0.0), (hp, LANES))
        acc_ref[...] = jnp.where(mask, jnp.exp(z - sp), 0.0) * vn_ref[0]

    head_row = lax.broadcasted_iota(jnp.int32, (hp, width), 0)
    head_lane = lax.broadcasted_iota(jnp.int32, (hp, width), 1) % heads
    own = head_row == head_lane
    r = lax.broadcasted_iota(jnp.int32, (LANES, 2 * LANES), 0)
    c = lax.broadcasted_iota(jnp.int32, (LANES, 2 * LANES), 1)
    neg_sum = jnp.where((c >= LANES) | ((r % heads == c % heads) & (r // heads >= c // heads)), -1.0, 0.0).astype(BF16)
    bias = bias_ref[...]

    order = list(reversed(range(pages_per_step)))
    zs = [_dot_nt(q, k_refs[i][0, 0].reshape(width, hd).astype(BF16)) + bias for i in order]
    sps = [jnp.where(own, _softplus(z), 0.0) for z in zs]
    stacked = jnp.concatenate(
        [sp[:, b * LANES:(b + 1) * LANES] for sp in sps for b in reversed(range(nblk))], axis=0)
    s_hi, s_lo = _split_bf16(stacked, 2)
    sums = _dot(s_hi, neg_sum) + _dot(s_lo, neg_sum)

    later = later_ref[...]
    acc = acc_ref[...]
    for n, i in enumerate(order):
        pieces = [None] * nblk
        for m, b in enumerate(reversed(range(nblk))):
            rows = slice((n * nblk + m) * hp, (n * nblk + m + 1) * hp)
            blk = slice(b * LANES, (b + 1) * LANES)
            pieces[b] = jnp.where(own[:, blk], jnp.exp(zs[n][:, blk] + sums[rows, :LANES] + later), 0.0)
            later = later + sums[rows, LANES:]
        w = jnp.concatenate(pieces, axis=1).astype(BF16)
        acc = acc + _dot(w, v_refs[i][0, 0].reshape(width, hd).astype(BF16))
    later_ref[...] = later
    acc_ref[...] = acc

    @pl.when(g == pl.num_programs(step_axis) - 1)
    def _():
        o_ref[0] = acc


def _hgrn_prompt_sb_sample_kernel(pt_ref, *refs, hgrn, sample):
    n_hg, n_sm = 7, 4 + 2 * sample["pages_per_step"]
    hg_in, sm_in = refs[:n_hg], refs[n_hg:n_hg + n_sm]
    o_hg, s_hg, o_sm, state_ref, later_ref, acc_ref = refs[n_hg + n_sm:]
    _hgrn_prompt_kernel(*hg_in, o_hg, s_hg, state_ref, **hgrn)
    _sb_sample_kernel(pt_ref, *sm_in, o_sm, later_ref, acc_ref, **sample, step_axis=2)


def hgrn_prompt_sb_sample(proj, lb, norm_w, q, bias_b, k_new, v_new, cache_k, cache_v, page_table, *, layer, batch, seq,
                          heads, key_dim, col0, chunk, hps, pages_per_step):
    nb, hp, hd = q.shape
    _, _, page, cache_heads, _ = cache_k.shape
    n_pages = page_table.shape[1]
    nc = seq // chunk
    head_blocks = heads // hps
    assert key_dim == LANES and hd == LANES and LANES % cache_heads == 0 and col0 % hps == 0
    assert nb == batch * head_blocks and n_pages == nc * pages_per_step
    lm = jnp.asarray(_level_mask_table(chunk))

    def seg(k):
        return pl.BlockSpec((chunk, hps * key_dim), lambda b, h, c, pt: (b * nc + c, (col0 + k * heads) // hps + h))

    whole = lambda shape: pl.BlockSpec(shape, lambda b, h, c, pt: (0,) * len(shape))

    def page_spec(i):
        def index(b, h, c, pt):
            return (layer, pt[b * head_blocks + h, (nc - 1 - c) * pages_per_step + i], 0, 0, 0)
        return pl.BlockSpec((1, 1, page, cache_heads, hd), index)

    per_sample = pl.BlockSpec((1, hp, hd), lambda b, h, c, pt: (b * head_blocks + h, 0, 0))
    grid_spec = pltpu.PrefetchScalarGridSpec(
        num_scalar_prefetch=1,
        grid=(batch, head_blocks, nc),
        in_specs=[seg(0), seg(1), seg(2), seg(3), whole((heads, key_dim)), whole((heads, key_dim)), whole((chunk, chunk)),
                  per_sample, whole((hp, page * cache_heads)), per_sample, per_sample]
        + [page_spec(i) for i in range(pages_per_step)] * 2,
        out_specs=[
            pl.BlockSpec((chunk, hps * key_dim), lambda b, h, c, pt: (b * nc + c, h)),
            pl.BlockSpec((1, hps, key_dim, key_dim), lambda b, h, c, pt: (b, h, 0, 0)),
            per_sample,
        ],
        scratch_shapes=[pltpu.VMEM((hps, key_dim, key_dim), F32), pltpu.VMEM((hp, LANES), F32),
                        pltpu.VMEM((hp, hd), F32)],
    )
    kernel = functools.partial(
        _hgrn_prompt_sb_sample_kernel,
        hgrn=dict(chunk=chunk, hps=hps),
        sample=dict(pages_per_step=pages_per_step, heads=cache_heads, page=page, past=n_pages * page, scale=hd ** -0.5))
    return pl.pallas_call(
        kernel,
        grid_spec=grid_spec,
        out_shape=[
            jax.ShapeDtypeStruct((batch * seq, heads * key_dim), BF16),
            jax.ShapeDtypeStruct((batch, heads, key_dim, key_dim), F32),
            jax.ShapeDtypeStruct((nb, hp, hd), F32),
        ],
        compiler_params=_params("parallel", "parallel", "arbitrary"),
        name="hgrn_prompt_sb_sample",
    )(page_table, proj, proj, proj, proj, lb, norm_w, lm, q, bias_b, k_new, v_new,
      *([cache_k] * pages_per_step), *([cache_v] * pages_per_step))


def _hgrn_sample_kernel(q_ref, f_ref, i_ref, g_ref, lb_ref, nw_ref, s_ref, o_ref, so_ref, *, samples):
    h = pl.program_id(0)
    kd = q_ref.shape[-1]
    lb = lb_ref[pl.ds(h, 1), :]
    qh, kh, logf = _hgrn_gates(q_ref[...], f_ref[...], lb, kd)
    vh = i_ref[...]
    f = jnp.exp(logf)
    qk = jnp.sum(qh * kh, axis=-1, keepdims=True)
    o_ref[...] = jnp.zeros_like(o_ref)
    for b in range(samples):
        row = slice(b, b + 1)
        state = s_ref[b, 0]
        f_col = jnp.broadcast_to(f[row], (kd, kd)).T
        k_col = jnp.broadcast_to(kh[row], (kd, kd)).T
        so_ref[b, 0] = f_col * state + k_col * vh[row]
        q_dec = jnp.broadcast_to(qh[row] * f[row], (SUBLANES, kd)).astype(BF16)
        o = _dot(q_dec, state.astype(BF16))[0:1] + qk[row] * vh[row]
        o_ref[row, :] = _hgrn_out(o, nw_ref[pl.ds(h, 1), :], g_ref[row, :])


def hgrn_sample(proj, lb, norm_w, state, *, layer, heads, key_dim, col0, samples):
    rows = proj.shape[0]

    def seg(k):
        return pl.BlockSpec((rows, key_dim), lambda h: (0, col0 + k * heads + h))

    whole = pl.BlockSpec((heads, key_dim), lambda h: (0, 0))
    return pl.pallas_call(
        functools.partial(_hgrn_sample_kernel, samples=samples),
        grid=(heads,),
        in_specs=[seg(0), seg(1), seg(2), seg(3), whole, whole,
                  pl.BlockSpec((None, samples, 1, key_dim, key_dim), lambda h: (layer, 0, h, 0, 0))],
        out_specs=[
            pl.BlockSpec((rows, key_dim), lambda h: (0, h)),
            pl.BlockSpec((samples, 1, key_dim, key_dim), lambda h: (0, h, 0, 0)),
        ],
        out_shape=[
            jax.ShapeDtypeStruct((rows, heads * key_dim), F32),
            jax.ShapeDtypeStruct((samples, heads, key_dim, key_dim), F32),
        ],
        compiler_params=_params("parallel"),
        name="hgrn_sample",
    )(proj, proj, proj, proj, lb, norm_w, state)


SAMPLE_ROWS = 16
TILES = dict(norm_tm=1024, in_tm=2048, in_tn=512, sb_tq=512, sb_tk=256, sb_pieces=1, hg_chunk=256, hg_heads=2,
             merge_tm=256, mlp_tm=512, mlp_tf=1024, sample_pages=8)


def kernel(x_prompt, x_sample, cache_k, cache_v, state_hgrn, page_table, norm_mix_pre, norm_mix_post,
           norm_mlp_pre, norm_mlp_post, w_in, sb_bias, lower_bounds, hgrn_norm, w_branch_a, w_branch_b, w_out,
           w_up, w_down):
    batch, seq, d_model = x_prompt.shape
    nb, dec_seq, _ = x_sample.shape
    depth, _, page, heads, head_dim = cache_k.shape
    assert dec_seq == 1
    sb_width = heads * head_dim
    key_width = lower_bounds.shape[1]
    key_dim = key_width // heads
    hg_col0 = 3 * sb_width // key_dim
    gate_off = 3 * sb_width + 4 * key_width
    t = TILES

    lb_all = jnp.cumsum(jax.nn.softmax(lower_bounds.astype(F32), axis=0), axis=0)
    lb_all = jnp.maximum(lb_all - lb_all[:1], 0.0).reshape(depth, heads, key_dim)
    norm_w = hgrn_norm.reshape(depth, heads, key_dim)
    w_a_b, w_b_b, w_o_b, w_up_b, w_down_b = (w.astype(BF16) for w in (w_branch_a, w_branch_b, w_out, w_up, w_down))

    xp = x_prompt.reshape(batch * seq, d_model)
    xs = jnp.pad(x_sample.reshape(nb, d_model), ((0, SAMPLE_ROWS - nb), (0, 0)))
    pad_heads = lambda a: jnp.pad(a.reshape(nb, heads, head_dim), ((0, 0), (0, SAMPLE_ROWS - heads), (0, 0)))
    s_prompt, k_sample, v_sample, s_sample = [], [], [], []
    k_buf = v_buf = None
    xn, xns = norm_cast(xp, xs, norm_mix_pre[0], tm=t["norm_tm"])
    for l in range(depth):
        proj, proj_s, k_buf, v_buf = in_proj(xn, xns, w_in, layer=l, tm=t["in_tm"], tn=t["in_tn"], col0=sb_width,
                                             width=sb_width, k_buf=k_buf, v_buf=v_buf)
        o_a = sb_prompt(proj, k_buf, v_buf, sb_bias[l], layer=l, batch=batch, seq=seq, heads=heads,
                        head_dim=head_dim, tq=t["sb_tq"], tk=t["sb_tk"], pieces=t["sb_pieces"])
        k_s = proj_s[:nb, sb_width:2 * sb_width]
        v_s = proj_s[:nb, 2 * sb_width:3 * sb_width]
        bias_b = jnp.pad(jnp.broadcast_to(sb_bias[l][:, None], (heads, page * heads)),
                         ((0, SAMPLE_ROWS - heads), (0, 0)))
        o_b, s_p, o_as = hgrn_prompt_sb_sample(
            proj, lb_all[l], norm_w[l], pad_heads(proj_s[:nb, :sb_width]), bias_b, pad_heads(k_s), pad_heads(v_s),
            cache_k, cache_v, page_table, layer=l, batch=batch, seq=seq, heads=heads, key_dim=key_dim, col0=hg_col0,
            chunk=t["hg_chunk"], hps=t["hg_heads"], pages_per_step=t["sample_pages"])
        o_as = jnp.pad(o_as[:, :heads].reshape(nb, sb_width), ((0, SAMPLE_ROWS - nb), (0, 0)))
        o_bs, s_s = hgrn_sample(proj_s, lb_all[l], norm_w[l], state_hgrn, layer=l, heads=heads, key_dim=key_dim,
                                col0=hg_col0, samples=nb)
        xp, xs = merge_out((o_a, o_b, proj, xp), (o_as, o_bs, proj_s, xs), w_a_b, w_b_b, w_o_b, norm_mix_post[l],
                           layer=l, tm=t["merge_tm"], gate_off=gate_off)
        xp, xs, *normed = mlp(xp, xs, norm_mlp_pre[l], w_up_b, w_down_b, norm_mlp_post[l],
                              norm_mix_pre[l + 1] if l + 1 < depth else None, layer=l, tm=t["mlp_tm"], tf=t["mlp_tf"])
        if normed:
            xn, xns = normed
        s_prompt.append(s_p)
        k_sample.append(k_s.reshape(nb, dec_seq, heads, head_dim))
        v_sample.append(v_s.reshape(nb, dec_seq, heads, head_dim))
        s_sample.append(s_s)

    k_p, v_p = (buf.reshape(depth, batch, seq, heads, head_dim) for buf in (k_buf, v_buf))
    return (xp.reshape(batch, seq, d_model), xs[:nb].reshape(nb, dec_seq, d_model), k_p, v_p, jnp.stack(s_prompt),
            jnp.stack(k_sample), jnp.stack(v_sample), jnp.stack(s_sample))
```

```python
import functools

import numpy as np
import jax
import jax.numpy as jnp
from jax import lax
from jax.experimental import pallas as pl
from jax.experimental.pallas import tpu as pltpu

F32 = jnp.float32
BF16 = jnp.bfloat16
RMS_EPS = 1e-6
LANES = 128
SUBLANES = 8
VMEM_LIMIT = 56 * 1024 * 1024

NT_DIMS = (((1,), (1,)), ((), ()))


def _dot(a, b):
    return jnp.dot(a, b, preferred_element_type=F32)


def _dot_nt(a, b):
    return lax.dot_general(a, b, NT_DIMS, preferred_element_type=F32)


def _sigmoid(x):
    return 1.0 / (1.0 + jnp.exp(-x))


EXP_CLAMP = 40.0


def _softplus(x):
    return jnp.maximum(x, jnp.log(1.0 + jnp.exp(jnp.minimum(x, EXP_CLAMP))))


def _rms_scale(x, w):
    return x * lax.rsqrt(jnp.mean(x * x, axis=-1, keepdims=True) + RMS_EPS) * w


def _split_bf16(x, parts):
    out = []
    for _ in range(parts - 1):
        hi = x.astype(BF16)
        out.append(hi)
        x = x - hi.astype(F32)
    out.append(x.astype(BF16))
    return out


def _params(*sem):
    return pltpu.CompilerParams(dimension_semantics=sem, vmem_limit_bytes=VMEM_LIMIT)


def _norm_cast_kernel(x_ref, xs_ref, g_ref, o_ref, os_ref):
    o_ref[...] = _rms_scale(x_ref[...], g_ref[...]).astype(BF16)
    os_ref[...] = _rms_scale(xs_ref[...], g_ref[...]).astype(BF16)


def norm_cast(x, xs, gain, *, tm):
    m, d = x.shape
    rows_s = xs.shape[0]
    sample = pl.BlockSpec((rows_s, d), lambda i: (0, 0))
    return pl.pallas_call(
        _norm_cast_kernel,
        grid=(m // tm,),
        in_specs=[pl.BlockSpec((tm, d), lambda i: (i, 0)), sample, pl.BlockSpec((1, d), lambda i: (0, 0))],
        out_specs=[pl.BlockSpec((tm, d), lambda i: (i, 0)), sample],
        out_shape=[jax.ShapeDtypeStruct((m, d), BF16), jax.ShapeDtypeStruct((rows_s, d), BF16)],
        compiler_params=_params("arbitrary"),
        name="norm_cast",
    )(x, xs, gain.reshape(1, d))


def _in_proj_kernel(x_ref, xs_ref, w_ref, *rest, kv_tiles):
    o_ref, os_ref, k_ref, v_ref = rest[-4:]
    i = pl.program_id(0)
    j = pl.program_id(1)
    w = w_ref[...].astype(BF16)
    res = _dot(x_ref[...], w)
    o_ref[...] = res

    @pl.when(i == 0)
    def _():
        os_ref[...] = _dot(xs_ref[...], w)

    k0, v0, n = kv_tiles

    @pl.when((j >= k0) & (j < k0 + n))
    def _():
        k_ref[...] = res

    @pl.when((j >= v0) & (j < v0 + n))
    def _():
        v_ref[...] = res


def in_proj(x, xs, w_all, *, layer, tm, tn, col0, width, k_buf, v_buf):
    m, d = x.shape
    rows_s = xs.shape[0]
    depth, _, n = w_all.shape
    nj = n // tn
    in_specs = [
        pl.BlockSpec((tm, d), lambda i, j: (i, 0)),
        pl.BlockSpec((rows_s, d), lambda i, j: (0, 0)),
        pl.BlockSpec((None, d, tn), lambda i, j: (layer, 0, j)),
    ]
    out_specs = [pl.BlockSpec((tm, tn), lambda i, j: (i, j)),
                 pl.BlockSpec((rows_s, tn), lambda i, j: (0, jnp.where(i == 0, j, nj - 1)))]
    out_shape = [jax.ShapeDtypeStruct((m, n), F32), jax.ShapeDtypeStruct((rows_s, n), F32)]
    args = [x, xs, w_all]
    nt = width // tn
    kv_tiles = (col0 // tn, (col0 + width) // tn, nt)
    for first in kv_tiles[:2]:
        index = functools.partial(lambda i, j, f: (layer, i, jnp.clip(j - f, 0, nt - 1)), f=first)
        out_specs.append(pl.BlockSpec((None, tm, tn), index))
        out_shape.append(jax.ShapeDtypeStruct((depth, m, width), F32))
    aliases = {}
    if k_buf is not None:
        in_specs += [pl.BlockSpec(memory_space=pl.ANY)] * 2
        args += [k_buf, v_buf]
        aliases = {3: 2, 4: 3}
    return pl.pallas_call(
        functools.partial(_in_proj_kernel, kv_tiles=kv_tiles),
        grid=(m // tm, nj),
        in_specs=in_specs,
        out_specs=out_specs,
        out_shape=out_shape,
        input_output_aliases=aliases,
        compiler_params=_params("arbitrary", "arbitrary"),
        name="in_proj",
    )(*args)


def _sb_prompt_kernel(bias_ref, q_ref, k_ref, v_ref, o_ref, kb_ref, vb_ref, zr_ref, z_ref, s_ref, in_ref, w_ref,
                      acc_ref, *, tq, tk, scale, pieces):
    h = pl.program_id(1)
    i = pl.program_id(2)
    n_sub = tq // tk
    assert n_sub == 2

    @pl.when(i == 0)
    def _():
        kb_ref[...] = k_ref[...].astype(BF16)
        vb_ref[...] = v_ref[...].astype(BF16)

    q = (q_ref[...] * scale).astype(BF16)
    bias = bias_ref[h]
    row = lax.broadcasted_iota(jnp.int32, (pieces * tk, tk), 0) % tk
    col = lax.broadcasted_iota(jnp.int32, (pieces * tk, tk), 1)
    neg_suffix = jnp.where(row >= col, -1.0, 0.0).astype(BF16)
    local_q = lax.broadcasted_iota(jnp.int32, (tq, tk), 0)
    local_k = lax.broadcasted_iota(jnp.int32, (tq, tk), 1)

    nblk = (i + 1) * n_sub
    ks_of = lambda n: pl.multiple_of((nblk - 1 - n) * tk, tk)
    mask_of = lambda n: (local_k + (tk if n == 0 else 0)) < local_q

    def st_scores(slot, n):
        zr_ref[slot] = _dot_nt(q, kb_ref[pl.ds(ks_of(n), tk), :]) + bias

    def st_softplus(slot, later, n=None):
        z = zr_ref[slot]
        sp = _softplus(z)
        if n is not None:
            sp = jnp.where(mask_of(n), sp, 0.0)
        z_ref[slot] = z + later
        s_ref[slot] = jnp.concatenate(_split_bf16(sp, pieces), axis=1)
        return later - jnp.sum(sp, axis=-1, keepdims=True)

    def st_suffix(slot):
        in_ref[slot] = _dot(s_ref[slot], neg_suffix)

    def st_weights(slot, n=None):
        w = jnp.exp(z_ref[slot] + in_ref[slot])
        if n is not None:
            w = jnp.where(mask_of(n), w, 0.0)
        w_ref[slot] = w.astype(BF16)

    def st_values(slot, n, acc):
        return acc + _dot(w_ref[slot], vb_ref[pl.ds(ks_of(n), tk), :])

    zero_later = jnp.zeros((tq, 1), F32)
    zero_acc = jnp.zeros((tq, LANES), F32)

    @pl.when(i < 2)
    def _():
        def block(n, later, acc, masked):
            st_scores(0, n)
            later = st_softplus(0, later, n if masked else None)
            st_suffix(0)
            st_weights(0, n if masked else None)
            return later, st_values(0, n, acc)

        carry = block(1, *block(0, zero_later, zero_acc, True), True)
        _, acc = lax.fori_loop(n_sub, nblk, lambda n, c: block(n, *c, False), carry)
        acc_ref[...] = acc

    @pl.when(i >= 2)
    def _():
        def trip(n, p, later, acc, first=-1, last=4):
            static = isinstance(n, int)
            if first < 0 <= last:
                acc = st_values(p, n, acc)
            if first < 1 <= last:
                st_weights(1 - p, n + 1 if static and n + 1 < n_sub else None)
            if first < 2 <= last:
                st_suffix(p)
            if first < 3 <= last:
                later = st_softplus(1 - p, later, n + 3 if static and n + 3 < n_sub else None)
            if first < 4 <= last:
                st_scores(p, n + 4)
            return later, acc

        c = (zero_later, zero_acc)
        for n in range(-4, 2):
            c = trip(n, n % 2, *c, first=-n - 1)

        def two_trips(pair, c):
            n = 2 * pair + 2
            return trip(n + 1, 1, *trip(n, 0, *c))

        c = lax.fori_loop(0, i - 2, two_trips, c)
        for d in range(4):
            c = trip(nblk - 4 + d, d % 2, *c, last=3 - d)
        acc_ref[...] = c[1]

    o_ref[...] = acc_ref[...].astype(o_ref.dtype)


def sb_prompt(proj, k_buf, v_buf, bias, *, layer, batch, seq, heads, head_dim, tq, tk, pieces):
    assert head_dim == LANES
    nq = seq // tq
    kernel = functools.partial(_sb_prompt_kernel, tq=tq, tk=tk, scale=head_dim ** -0.5, pieces=pieces)
    kv_spec = pl.BlockSpec((None, seq, head_dim), lambda b, h, i: (layer, b, h))
    return pl.pallas_call(
        kernel,
        grid=(batch, heads, nq),
        in_specs=[
            pl.BlockSpec(memory_space=pltpu.SMEM),
            pl.BlockSpec((tq, head_dim), lambda b, h, i: (b * nq + i, h)),
            kv_spec, kv_spec,
        ],
        out_specs=pl.BlockSpec((tq, head_dim), lambda b, h, i: (b * nq + i, h)),
        out_shape=jax.ShapeDtypeStruct((batch * seq, heads * head_dim), BF16),
        scratch_shapes=[
            pltpu.VMEM((seq, head_dim), BF16), pltpu.VMEM((seq, head_dim), BF16),
            pltpu.VMEM((2, tq, tk), F32), pltpu.VMEM((2, tq, tk), F32), pltpu.VMEM((2, tq, pieces * tk), BF16),
            pltpu.VMEM((2, tq, tk), F32), pltpu.VMEM((2, tq, tk), BF16), pltpu.VMEM((tq, head_dim), F32),
        ],
        compiler_params=_params("parallel", "parallel", "arbitrary"),
        name="sb_prompt",
    )(bias, proj, k_buf, v_buf)


def _hgrn_gates(qb, fpre, lb, key_dim):
    e = jnp.exp(-jnp.abs(fpre))
    r = 1.0 / (1.0 + e)
    log_sig = jnp.minimum(fpre, 0.0) + jnp.log(r)
    sig_neg = jnp.where(fpre >= 0.0, e * r, r)
    a = jnp.log(lb)
    b = jnp.log(1.0 - lb) + log_sig
    logf = jnp.maximum(a, b) + jnp.log(1.0 + jnp.exp(-jnp.abs(a - b)))
    kh = (1.0 - lb) * sig_neg
    qh = qb * _sigmoid(qb) * (key_dim ** -0.5)
    return qh, kh, logf


def _hgrn_out(o, norm_w, gate):
    return _rms_scale(o, norm_w) * (gate * _sigmoid(gate))


def _level_mask_table(chunk):
    t = np.arange(chunk)[:, None]
    s = np.arange(chunk)[None, :]
    table = np.full((chunk, chunk), -1, np.int32)
    m, lvl = SUBLANES, 0
    while m < chunk:
        sibling = (t // (2 * m) == s // (2 * m)) & (t % (2 * m) >= m) & (s % (2 * m) < m)
        table[sibling] = lvl
        m, lvl = 2 * m, lvl + 1
    return table


def _hgrn_prompt_kernel(q_ref, f_ref, i_ref, g_ref, lb_ref, nw_ref, lm_ref, o_ref, s_ref, state_ref, *, chunk, hps):
    hb = pl.program_id(1)
    ci = pl.program_id(2)
    kd = q_ref.shape[-1] // hps

    @pl.when(ci == 0)
    def _():
        state_ref[...] = jnp.zeros_like(state_ref)

    row = lax.broadcasted_iota(jnp.int32, (chunk, chunk), 0)
    col = lax.broadcasted_iota(jnp.int32, (chunk, chunk), 1)
    prefix_incl = jnp.where(col <= row, 1.0, 0.0).astype(BF16)
    rows = lax.broadcasted_iota(jnp.int32, (chunk, kd), 0)
    lm = lm_ref[...]
    groups = chunk // SUBLANES
    sub = lax.broadcasted_iota(jnp.int32, (groups, SUBLANES, kd), 1)

    for hh in range(hps):
        lanes = slice(hh * kd, (hh + 1) * kd)
        lb = lb_ref[pl.ds(hb * hps + hh, 1), :]
        qh, kh, logf = _hgrn_gates(q_ref[:, lanes], f_ref[:, lanes], lb, kd)
        vh = i_ref[:, lanes]
        vh_b = vh.astype(BF16)

        b = sum(_dot(prefix_incl, part) for part in _split_bf16(logf, 3))

        state = state_ref[hh]
        o = _dot((qh * jnp.exp(b)).astype(BF16), state.astype(BF16))

        scores = jnp.zeros((chunk, chunk), F32)
        m, lvl = SUBLANES, 0
        while m < chunk:
            bnd = jnp.concatenate(
                [jnp.broadcast_to(b[p + m - 1:p + m, :], (2 * m, kd)) for p in range(0, chunk, 2 * m)], axis=0)
            second = (rows & (2 * m - 1)) >= m
            d = b - bnd
            qt = jnp.where(second, qh * jnp.exp(d), 0.0)
            kt = jnp.where(second, 0.0, kh * jnp.exp(-d))
            scores = jnp.where(lm == lvl, _dot_nt(qt.astype(BF16), kt.astype(BF16)), scores)
            m, lvl = 2 * m, lvl + 1
        o = o + _dot(scores.astype(BF16), vh_b)

        q3 = qh.reshape(groups, SUBLANES, kd)
        k3 = kh.reshape(groups, SUBLANES, kd)
        b3 = b.reshape(groups, SUBLANES, kd)
        v3 = vh.reshape(groups, SUBLANES, kd)
        o3 = jnp.zeros((groups, SUBLANES, kd), F32)
        for s in range(SUBLANES):
            decay = jnp.exp(b3 - b3[:, s:s + 1, :])
            a = jnp.sum(q3 * k3[:, s:s + 1, :] * decay, axis=-1, keepdims=True)
            o3 = o3 + jnp.where(sub >= s, a, 0.0) * v3[:, s:s + 1, :]
        o = o + o3.reshape(chunk, kd)

        b_last = b[chunk - 1:chunk, :]
        k_dec = kh * jnp.exp(b_last - b)
        decay_col = jnp.broadcast_to(jnp.exp(b_last), (kd, kd)).T
        state_ref[hh] = decay_col * state + _dot(k_dec.T.astype(BF16), vh_b)

        o_ref[:, lanes] = _hgrn_out(o, nw_ref[pl.ds(hb * hps + hh, 1), :], g_ref[:, lanes]).astype(o_ref.dtype)

    @pl.when(ci == pl.num_programs(2) - 1)
    def _():
        s_ref[0] = state_ref[...]


def _merge_out_kernel(*refs, gate_blocks):
    n_in = 3 + 2 * gate_blocks
    prompt, sample = refs[:n_in], refs[n_in:2 * n_in]
    wa_ref, wb_ref, wo_ref, g_ref, o_ref, os_ref = refs[2 * n_in:]

    rows = lambda k, dtype=F32: jnp.concatenate([prompt[k][...].astype(dtype), sample[k][...].astype(dtype)], axis=0)
    gate = lambda ks: jnp.concatenate([_sigmoid(rows(k)) for k in ks], axis=1)
    a = _dot(rows(0, BF16), wa_ref[...])
    b = _dot(rows(1, BF16), wb_ref[...])
    merged = gate(range(2, 2 + gate_blocks)) * a + gate(range(2 + gate_blocks, 2 + 2 * gate_blocks)) * b
    y = _dot(merged.astype(BF16), wo_ref[...])
    out = rows(n_in - 1) + _rms_scale(y, g_ref[...])
    tm = o_ref.shape[0]
    o_ref[...] = out[:tm]
    os_ref[...] = out[tm:]


def merge_out(prompt, sample, wa_all, wb_all, wo_all, gain, *, layer, tm, gate_off):
    m, d = prompt[3].shape
    rows_s = sample[3].shape[0]
    wa_rows, wb_rows = wa_all.shape[1], wb_all.shape[1]
    gw = int(np.gcd(gate_off, d))
    gate_blocks = d // gw
    const = lambda *shape: pl.BlockSpec((None, *shape), lambda i: (layer, 0, 0), pipeline_mode=pl.Buffered(1))

    def group(rows, row_index):
        at = lambda c: (lambda i: (row_index(i), c))
        return ([pl.BlockSpec((rows, wa_rows), at(0)), pl.BlockSpec((rows, wb_rows), at(0))]
                + [pl.BlockSpec((rows, gw), at(gate_off // gw + c)) for c in range(2 * gate_blocks)]
                + [pl.BlockSpec((rows, d), at(0))])

    operands = lambda t: (t[0], t[1], *([t[2]] * (2 * gate_blocks)), t[3])
    return pl.pallas_call(
        functools.partial(_merge_out_kernel, gate_blocks=gate_blocks),
        grid=(m // tm,),
        in_specs=group(tm, lambda i: i) + group(rows_s, lambda i: 0)
        + [const(wa_rows, d), const(wb_rows, d), const(d, d), pl.BlockSpec((1, d), lambda i: (0, 0))],
        out_specs=[pl.BlockSpec((tm, d), lambda i: (i, 0)), pl.BlockSpec((rows_s, d), lambda i: (0, 0))],
        out_shape=[jax.ShapeDtypeStruct((m, d), F32), jax.ShapeDtypeStruct((rows_s, d), F32)],
        compiler_params=_params("arbitrary"),
        name="merge_out_proj",
    )(*operands(prompt), *operands(sample), wa_all, wb_all, wo_all, gain.reshape(1, d))


def _mlp_kernel(x_ref, xs_ref, gpre_ref, wu_ref, wd_ref, gpost_ref, *rest, feeds_next):
    if feeds_next:
        gnext_ref, o_ref, os_ref, xn_ref, xns_ref, hn_ref, acc_ref = rest
    else:
        o_ref, os_ref, hn_ref, acc_ref = rest
    j = pl.program_id(1)
    tm = x_ref.shape[0]

    @pl.when(j == 0)
    def _():
        hn_ref[:tm, :] = _rms_scale(x_ref[...], gpre_ref[...]).astype(BF16)
        hn_ref[tm:, :] = _rms_scale(xs_ref[...], gpre_ref[...]).astype(BF16)
        acc_ref[...] = jnp.zeros_like(acc_ref)

    u = jnp.maximum(_dot(hn_ref[...], wu_ref[...]), 0.0)
    acc_ref[...] += _dot((u * u).astype(BF16), wd_ref[...])

    @pl.when(j == pl.num_programs(1) - 1)
    def _():
        out = x_ref[...] + _rms_scale(acc_ref[:tm, :], gpost_ref[...])
        outs = xs_ref[...] + _rms_scale(acc_ref[tm:, :], gpost_ref[...])
        o_ref[...] = out
        os_ref[...] = outs
        if feeds_next:
            xn_ref[...] = _rms_scale(out, gnext_ref[...]).astype(BF16)
            xns_ref[...] = _rms_scale(outs, gnext_ref[...]).astype(BF16)


def mlp(x, xs, gpre, wu_all, wd_all, gpost, gnext, *, layer, tm, tf):
    m, d = x.shape
    rows_s = xs.shape[0]
    f = wu_all.shape[2]
    vec = pl.BlockSpec((1, d), lambda i, j: (0, 0))
    rows = pl.BlockSpec((tm, d), lambda i, j: (i, 0))
    sample = pl.BlockSpec((rows_s, d), lambda i, j: (0, 0))
    feeds_next = gnext is not None
    out_specs = [rows, sample] + ([rows, sample] if feeds_next else [])
    out_shape = [jax.ShapeDtypeStruct((m, d), F32), jax.ShapeDtypeStruct((rows_s, d), F32)]
    if feeds_next:
        out_shape += [jax.ShapeDtypeStruct((m, d), BF16), jax.ShapeDtypeStruct((rows_s, d), BF16)]
    return pl.pallas_call(
        functools.partial(_mlp_kernel, feeds_next=feeds_next),
        grid=(m // tm, f // tf),
        in_specs=[
            rows, sample, vec,
            pl.BlockSpec((None, d, tf), lambda i, j: (layer, 0, j)),
            pl.BlockSpec((None, tf, d), lambda i, j: (layer, j, 0)),
            vec,
        ] + ([vec] if feeds_next else []),
        out_specs=out_specs,
        out_shape=out_shape,
        scratch_shapes=[pltpu.VMEM((tm + rows_s, d), BF16), pltpu.VMEM((tm + rows_s, d), F32)],
        compiler_params=_params("arbitrary", "arbitrary"),
        name="mlp",
    )(x, xs, gpre.reshape(1, d), wu_all, wd_all, gpost.reshape(1, d), *([gnext.reshape(1, d)] if feeds_next else []))


def _sb_sample_kernel(pt_ref, q_ref, bias_ref, kn_ref, vn_ref, *rest, pages_per_step, heads, page, past, scale,
                      step_axis):
    del pt_ref
    k_refs = rest[:pages_per_step]
    v_refs = rest[pages_per_step:2 * pages_per_step]
    o_ref, later_ref, acc_ref = rest[2 * pages_per_step:]
    g = pl.program_id(step_axis)
    hp, hd = q_ref.shape[1], q_ref.shape[2]
    width = page * heads
    nblk = width // LANES

    qs = q_ref[0] * scale
    q = qs.astype(BF16)

    @pl.when(g == 0)
    def _():
        z = jnp.sum(qs * kn_ref[0], axis=-1, keepdims=True) + bias_ref[:, 0:1]
        k_pos = past + lax.broadcasted_iota(jnp.int32, (hp, 1), 1)
        mask = k_pos < past
        sp = _softplus(z)
        later_ref[...] = jnp.broadcast_to(jnp.where(mask, -sp, 0.0), (hp, LANES))
        acc_ref[...] = jnp.where(mask, jnp.exp(z - sp), 0.0) * vn_ref[0]

    head_row = lax.broadcasted_iota(jnp.int32, (hp, width), 0)
    head_lane = lax.broadcasted_iota(jnp.int32, (hp, width), 1) % heads
    own = head_row == head_lane
    r = lax.broadcasted_iota(jnp.int32, (LANES, 2 * LANES), 0)
    c = lax.broadcasted_iota(jnp.int32, (LANES, 2 * LANES), 1)
    neg_sum = jnp.where((c >= LANES) | ((r % heads == c % heads) & (r // heads >= c // heads)), -1.0, 0.0).astype(BF16)
    bias = bias_ref[...]

    order = list(reversed(range(pages_per_step)))
    zs = [_dot_nt(q, k_refs[i][0, 0].reshape(width, hd).astype(BF16)) + bias for i in order]
    sps = [jnp.where(own, _softplus(z), 0.0) for z in zs]
    stacked = jnp.concatenate(
        [sp[:, b * LANES:(b + 1) * LANES] for sp in sps for b in reversed(range(nblk))], axis=0)
    s_hi, s_lo = _split_bf16(stacked, 2)
    sums = _dot(s_hi, neg_sum) + _dot(s_lo, neg_sum)

    later = later_ref[...]
    acc = acc_ref[...]
    for n, i in enumerate(order):
        pieces = [None] * nblk
        for m, b in enumerate(reversed(range(nblk))):
            rows = slice((n * nblk + m) * hp, (n * nblk + m + 1) * hp)
            blk = slice(b * LANES, (b + 1) * LANES)
            pieces[b] = jnp.where(own[:, blk], jnp.exp(zs[n][:, blk] + sums[rows, :LANES] + later), 0.0)
            later = later + sums[rows, LANES:]
        w = jnp.concatenate(pieces, axis=1).astype(BF16)
        acc = acc + _dot(w, v_refs[i][0, 0].reshape(width, hd).astype(BF16))
    later_ref[...] = later
    acc_ref[...] = acc

    @pl.when(g == pl.num_programs(step_axis) - 1)
    def _():
        o_ref[0] = acc


def _hgrn_prompt_sb_sample_kernel(pt_ref, *refs, hgrn, sample):
    n_hg, n_sm = 7, 4 + 2 * sample["pages_per_step"]
    hg_in, sm_in = refs[:n_hg], refs[n_hg:n_hg + n_sm]
    o_hg, s_hg, o_sm, state_ref, later_ref, acc_ref = refs[n_hg + n_sm:]
    _hgrn_prompt_kernel(*hg_in, o_hg, s_hg, state_ref, **hgrn)
    _sb_sample_kernel(pt_ref, *sm_in, o_sm, later_ref, acc_ref, **sample, step_axis=2)


def hgrn_prompt_sb_sample(proj, lb, norm_w, q, bias_b, k_new, v_new, cache_k, cache_v, page_table, *, layer, batch, seq,
                          heads, key_dim, col0, chunk, hps, pages_per_step):
    nb, hp, hd = q.shape
    _, _, page, cache_heads, _ = cache_k.shape
    n_pages = page_table.shape[1]
    nc = seq // chunk
    head_blocks = heads // hps
    assert key_dim == LANES and hd == LANES and LANES % cache_heads == 0 and col0 % hps == 0
    assert nb == batch * head_blocks and n_pages == nc * pages_per_step
    lm = jnp.asarray(_level_mask_table(chunk))

    def seg(k):
        return pl.BlockSpec((chunk, hps * key_dim), lambda b, h, c, pt: (b * nc + c, (col0 + k * heads) // hps + h))

    whole = lambda shape: pl.BlockSpec(shape, lambda b, h, c, pt: (0,) * len(shape))

    def page_spec(i):
        def index(b, h, c, pt):
            return (layer, pt[b * head_blocks + h, (nc - 1 - c) * pages_per_step + i], 0, 0, 0)
        return pl.BlockSpec((1, 1, page, cache_heads, hd), index)

    per_sample = pl.BlockSpec((1, hp, hd), lambda b, h, c, pt: (b * head_blocks + h, 0, 0))
    grid_spec = pltpu.PrefetchScalarGridSpec(
        num_scalar_prefetch=1,
        grid=(batch, head_blocks, nc),
        in_specs=[seg(0), seg(1), seg(2), seg(3), whole((heads, key_dim)), whole((heads, key_dim)), whole((chunk, chunk)),
                  per_sample, whole((hp, page * cache_heads)), per_sample, per_sample]
        + [page_spec(i) for i in range(pages_per_step)] * 2,
        out_specs=[
            pl.BlockSpec((chunk, hps * key_dim), lambda b, h, c, pt: (b * nc + c, h)),
            pl.BlockSpec((1, hps, key_dim, key_dim), lambda b, h, c, pt: (b, h, 0, 0)),
            per_sample,
        ],
        scratch_shapes=[pltpu.VMEM((hps, key_dim, key_dim), F32), pltpu.VMEM((hp, LANES), F32),
                        pltpu.VMEM((hp, hd), F32)],
    )
    kernel = functools.partial(
        _hgrn_prompt_sb_sample_kernel,
        hgrn=dict(chunk=chunk, hps=hps),
        sample=dict(pages_per_step=pages_per_step, heads=cache_heads, page=page, past=n_pages * page, scale=hd ** -0.5))
    return pl.pallas_call(
        kernel,
        grid_spec=grid_spec,
        out_shape=[
            jax.ShapeDtypeStruct((batch * seq, heads * key_dim), BF16),
            jax.ShapeDtypeStruct((batch, heads, key_dim, key_dim), F32),
            jax.ShapeDtypeStruct((nb, hp, hd), F32),
        ],
        compiler_params=_params("parallel", "parallel", "arbitrary"),
        name="hgrn_prompt_sb_sample",
    )(page_table, proj, proj, proj, proj, lb, norm_w, lm, q, bias_b, k_new, v_new,
      *([cache_k] * pages_per_step), *([cache_v] * pages_per_step))


def _hgrn_sample_kernel(q_ref, f_ref, i_ref, g_ref, lb_ref, nw_ref, s_ref, o_ref, so_ref, *, samples):
    h = pl.program_id(0)
    kd = q_ref.shape[-1]
    lb = lb_ref[pl.ds(h, 1), :]
    qh, kh, logf = _hgrn_gates(q_ref[...], f_ref[...], lb, kd)
    vh = i_ref[...]
    f = jnp.exp(logf)
    qk = jnp.sum(qh * kh, axis=-1, keepdims=True)
    o_ref[...] = jnp.zeros_like(o_ref)
    for b in range(samples):
        row = slice(b, b + 1)
        state = s_ref[b, 0]
        f_col = jnp.broadcast_to(f[row], (kd, kd)).T
        k_col = jnp.broadcast_to(kh[row], (kd, kd)).T
        so_ref[b, 0] = f_col * state + k_col * vh[row]
        q_dec = jnp.broadcast_to(qh[row] * f[row], (SUBLANES, kd)).astype(BF16)
        o = _dot(q_dec, state.astype(BF16))[0:1] + qk[row] * vh[row]
        o_ref[row, :] = _hgrn_out(o, nw_ref[pl.ds(h, 1), :], g_ref[row, :])


def hgrn_sample(proj, lb, norm_w, state, *, layer, heads, key_dim, col0, samples):
    rows = proj.shape[0]

    def seg(k):
        return pl.BlockSpec((rows, key_dim), lambda h: (0, col0 + k * heads + h))

    whole = pl.BlockSpec((heads, key_dim), lambda h: (0, 0))
    return pl.pallas_call(
        functools.partial(_hgrn_sample_kernel, samples=samples),
        grid=(heads,),
        in_specs=[seg(0), seg(1), seg(2), seg(3), whole, whole,
                  pl.BlockSpec((None, samples, 1, key_dim, key_dim), lambda h: (layer, 0, h, 0, 0))],
        out_specs=[
            pl.BlockSpec((rows, key_dim), lambda h: (0, h)),
            pl.BlockSpec((samples, 1, key_dim, key_dim), lambda h: (0, h, 0, 0)),
        ],
        out_shape=[
            jax.ShapeDtypeStruct((rows, heads * key_dim), F32),
            jax.ShapeDtypeStruct((samples, heads, key_dim, key_dim), F32),
        ],
        compiler_params=_params("parallel"),
        name="hgrn_sample",
    )(proj, proj, proj, proj, lb, norm_w, state)


SAMPLE_ROWS = 16
TILES = dict(norm_tm=1024, in_tm=2048, in_tn=512, sb_tq=512, sb_tk=256, sb_pieces=1, hg_chunk=256, hg_heads=2,
             merge_tm=256, mlp_tm=512, mlp_tf=1024, sample_pages=8)


def kernel(x_prompt, x_sample, cache_k, cache_v, state_hgrn, page_table, norm_mix_pre, norm_mix_post,
           norm_mlp_pre, norm_mlp_post, w_in, sb_bias, lower_bounds, hgrn_norm, w_branch_a, w_branch_b, w_out,
           w_up, w_down):
    batch, seq, d_model = x_prompt.shape
    nb, dec_seq, _ = x_sample.shape
    depth, _, page, heads, head_dim = cache_k.shape
    assert dec_seq == 1
    sb_width = heads * head_dim
    key_width = lower_bounds.shape[1]
    key_dim = key_width // heads
    hg_col0 = 3 * sb_width // key_dim
    gate_off = 3 * sb_width + 4 * key_width
    t = TILES

    lb_all = jnp.cumsum(jax.nn.softmax(lower_bounds.astype(F32), axis=0), axis=0)
    lb_all = jnp.maximum(lb_all - lb_all[:1], 0.0).reshape(depth, heads, key_dim)
    norm_w = hgrn_norm.reshape(depth, heads, key_dim)
    w_a_b, w_b_b, w_o_b, w_up_b, w_down_b = (w.astype(BF16) for w in (w_branch_a, w_branch_b, w_out, w_up, w_down))

    xp = x_prompt.reshape(batch * seq, d_model)
    xs = jnp.pad(x_sample.reshape(nb, d_model), ((0, SAMPLE_ROWS - nb), (0, 0)))
    pad_heads = lambda a: jnp.pad(a.reshape(nb, heads, head_dim), ((0, 0), (0, SAMPLE_ROWS - heads), (0, 0)))
    s_prompt, k_sample, v_sample, s_sample = [], [], [], []
    k_buf = v_buf = None
    xn, xns = norm_cast(xp, xs, norm_mix_pre[0], tm=t["norm_tm"])
    for l in range(depth):
        proj, proj_s, k_buf, v_buf = in_proj(xn, xns, w_in, layer=l, tm=t["in_tm"], tn=t["in_tn"], col0=sb_width,
                                             width=sb_width, k_buf=k_buf, v_buf=v_buf)
        o_a = sb_prompt(proj, k_buf, v_buf, sb_bias[l], layer=l, batch=batch, seq=seq, heads=heads,
                        head_dim=head_dim, tq=t["sb_tq"], tk=t["sb_tk"], pieces=t["sb_pieces"])
        k_s = proj_s[:nb, sb_width:2 * sb_width]
        v_s = proj_s[:nb, 2 * sb_width:3 * sb_width]
        bias_b = jnp.pad(jnp.broadcast_to(sb_bias[l][:, None], (heads, page * heads)),
                         ((0, SAMPLE_ROWS - heads), (0, 0)))
        o_b, s_p, o_as = hgrn_prompt_sb_sample(
            proj, lb_all[l], norm_w[l], pad_heads(proj_s[:nb, :sb_width]), bias_b, pad_heads(k_s), pad_heads(v_s),
            cache_k, cache_v, page_table, layer=l, batch=batch, seq=seq, heads=heads, key_dim=key_dim, col0=hg_col0,
            chunk=t["hg_chunk"], hps=t["hg_heads"], pages_per_step=t["sample_pages"])
        o_as = jnp.pad(o_as[:, :heads].reshape(nb, sb_width), ((0, SAMPLE_ROWS - nb), (0, 0)))
        o_bs, s_s = hgrn_sample(proj_s, lb_all[l], norm_w[l], state_hgrn, layer=l, heads=heads, key_dim=key_dim,
                                col0=hg_col0, samples=nb)
        xp, xs = merge_out((o_a, o_b, proj, xp), (o_as, o_bs, proj_s, xs), w_a_b, w_b_b, w_o_b, norm_mix_post[l],
                           layer=l, tm=t["merge_tm"], gate_off=gate_off)
        xp, xs, *normed = mlp(xp, xs, norm_mlp_pre[l], w_up_b, w_down_b, norm_mlp_post[l],
                              norm_mix_pre[l + 1] if l + 1 < depth else None, layer=l, tm=t["mlp_tm"], tf=t["mlp_tf"])
        if normed:
            xn, xns = normed
        s_prompt.append(s_p)
        k_sample.append(k_s.reshape(nb, dec_seq, heads, head_dim))
        v_sample.append(v_s.reshape(nb, dec_seq, heads, head_dim))
        s_sample.append(s_s)

    k_p, v_p = (buf.reshape(depth, batch, seq, heads, head_dim) for buf in (k_buf, v_buf))
    return (xp.reshape(batch, seq, d_model), xs[:nb].reshape(nb, dec_seq, d_model), k_p, v_p, jnp.stack(s_prompt),
            jnp.stack(k_sample), jnp.stack(v_sample), jnp.stack(s_sample))
```

```python
import functools

import numpy as np
import jax
import jax.numpy as jnp
from jax import lax
from jax.experimental import pallas as pl
from jax.experimental.pallas import tpu as pltpu

F32 = jnp.float32
BF16 = jnp.bfloat16
RMS_EPS = 1e-6
LANES = 128
SUBLANES = 8
VMEM_LIMIT = 56 * 1024 * 1024

NT_DIMS = (((1,), (1,)), ((), ()))


def _dot(a, b):
    return jnp.dot(a, b, preferred_element_type=F32)


def _dot_nt(a, b):
    return lax.dot_general(a, b, NT_DIMS, preferred_element_type=F32)


def _sigmoid(x):
    return 1.0 / (1.0 + jnp.exp(-x))


EXP_CLAMP = 40.0


def _softplus(x):
    return jnp.maximum(x, jnp.log(1.0 + jnp.exp(jnp.minimum(x, EXP_CLAMP))))


def _rms_scale(x, w):
    return x * lax.rsqrt(jnp.mean(x * x, axis=-1, keepdims=True) + RMS_EPS) * w


def _split_bf16(x, parts):
    out = []
    for _ in range(parts - 1):
        hi = x.astype(BF16)
        out.append(hi)
        x = x - hi.astype(F32)
    out.append(x.astype(BF16))
    return out


def _params(*sem):
    return pltpu.CompilerParams(dimension_semantics=sem, vmem_limit_bytes=VMEM_LIMIT)


def _norm_cast_kernel(x_ref, xs_ref, g_ref, o_ref, os_ref):
    o_ref[...] = _rms_scale(x_ref[...], g_ref[...]).astype(BF16)
    os_ref[...] = _rms_scale(xs_ref[...], g_ref[...]).astype(BF16)


def norm_cast(x, xs, gain, *, tm):
    m, d = x.shape
    rows_s = xs.shape[0]
    sample = pl.BlockSpec((rows_s, d), lambda i: (0, 0))
    return pl.pallas_call(
        _norm_cast_kernel,
        grid=(m // tm,),
        in_specs=[pl.BlockSpec((tm, d), lambda i: (i, 0)), sample, pl.BlockSpec((1, d), lambda i: (0, 0))],
        out_specs=[pl.BlockSpec((tm, d), lambda i: (i, 0)), sample],
        out_shape=[jax.ShapeDtypeStruct((m, d), BF16), jax.ShapeDtypeStruct((rows_s, d), BF16)],
        compiler_params=_params("arbitrary"),
        name="norm_cast",
    )(x, xs, gain.reshape(1, d))


def _in_proj_kernel(x_ref, xs_ref, w_ref, *rest, kv_tiles):
    o_ref, os_ref, k_ref, v_ref = rest[-4:]
    i = pl.program_id(0)
    j = pl.program_id(1)
    w = w_ref[...].astype(BF16)
    res = _dot(x_ref[...], w)
    o_ref[...] = res

    @pl.when(i == 0)
    def _():
        os_ref[...] = _dot(xs_ref[...], w)

    k0, v0, n = kv_tiles

    @pl.when((j >= k0) & (j < k0 + n))
    def _():
        k_ref[...] = res

    @pl.when((j >= v0) & (j < v0 + n))
    def _():
        v_ref[...] = res


def in_proj(x, xs, w_all, *, layer, tm, tn, col0, width, k_buf, v_buf):
    m, d = x.shape
    rows_s = xs.shape[0]
    depth, _, n = w_all.shape
    nj = n // tn
    in_specs = [
        pl.BlockSpec((tm, d), lambda i, j: (i, 0)),
        pl.BlockSpec((rows_s, d), lambda i, j: (0, 0)),
        pl.BlockSpec((None, d, tn), lambda i, j: (layer, 0, j)),
    ]
    out_specs = [pl.BlockSpec((tm, tn), lambda i, j: (i, j)),
                 pl.BlockSpec((rows_s, tn), lambda i, j: (0, jnp.where(i == 0, j, nj - 1)))]
    out_shape = [jax.ShapeDtypeStruct((m, n), F32), jax.ShapeDtypeStruct((rows_s, n), F32)]
    args = [x, xs, w_all]
    nt = width // tn
    kv_tiles = (col0 // tn, (col0 + width) // tn, nt)
    for first in kv_tiles[:2]:
        index = functools.partial(lambda i, j, f: (layer, i, jnp.clip(j - f, 0, nt - 1)), f=first)
        out_specs.append(pl.BlockSpec((None, tm, tn), index))
        out_shape.append(jax.ShapeDtypeStruct((depth, m, width), F32))
    aliases = {}
    if k_buf is not None:
        in_specs += [pl.BlockSpec(memory_space=pl.ANY)] * 2
        args += [k_buf, v_buf]
        aliases = {3: 2, 4: 3}
    return pl.pallas_call(
        functools.partial(_in_proj_kernel, kv_tiles=kv_tiles),
        grid=(m // tm, nj),
        in_specs=in_specs,
        out_specs=out_specs,
        out_shape=out_shape,
        input_output_aliases=aliases,
        compiler_params=_params("arbitrary", "arbitrary"),
        name="in_proj",
    )(*args)


def _sb_prompt_kernel(bias_ref, q_ref, k_ref, v_ref, wu_ref, wd_ref, o_ref, wub_ref, wdb_ref, kb_ref, vb_ref, zr_ref,
                      z_ref, s_ref, in_ref, w_ref, acc_ref, *, tq, tk, scale, pieces):
    wub_ref[...] = wu_ref[...].astype(BF16)
    wdb_ref[...] = wd_ref[...].astype(BF16)
    h = pl.program_id(1)
    i = pl.program_id(2)
    n_sub = tq // tk
    assert n_sub == 2

    @pl.when(i == 0)
    def _():
        kb_ref[...] = k_ref[...].astype(BF16)
        vb_ref[...] = v_ref[...].astype(BF16)

    q = (q_ref[...] * scale).astype(BF16)
    bias = bias_ref[h]
    row = lax.broadcasted_iota(jnp.int32, (pieces * tk, tk), 0) % tk
    col = lax.broadcasted_iota(jnp.int32, (pieces * tk, tk), 1)
    neg_suffix = jnp.where(row >= col, -1.0, 0.0).astype(BF16)
    local_q = lax.broadcasted_iota(jnp.int32, (tq, tk), 0)
    local_k = lax.broadcasted_iota(jnp.int32, (tq, tk), 1)

    nblk = (i + 1) * n_sub
    ks_of = lambda n: pl.multiple_of((nblk - 1 - n) * tk, tk)
    mask_of = lambda n: (local_k + (tk if n == 0 else 0)) < local_q

    def st_scores(slot, n):
        zr_ref[slot] = _dot_nt(q, kb_ref[pl.ds(ks_of(n), tk), :]) + bias

    def st_softplus(slot, later, n=None):
        z = zr_ref[slot]
        sp = _softplus(z)
        if n is not None:
            sp = jnp.where(mask_of(n), sp, 0.0)
        z_ref[slot] = z + later
        s_ref[slot] = jnp.concatenate(_split_bf16(sp, pieces), axis=1)
        return later - jnp.sum(sp, axis=-1, keepdims=True)

    def st_suffix(slot):
        in_ref[slot] = _dot(s_ref[slot], neg_suffix)

    def st_weights(slot, n=None):
        w = jnp.exp(z_ref[slot] + in_ref[slot])
        if n is not None:
            w = jnp.where(mask_of(n), w, 0.0)
        w_ref[slot] = w.astype(BF16)

    def st_values(slot, n, acc):
        return acc + _dot(w_ref[slot], vb_ref[pl.ds(ks_of(n), tk), :])

    zero_later = jnp.zeros((tq, 1), F32)
    zero_acc = jnp.zeros((tq, LANES), F32)

    @pl.when(i < 2)
    def _():
        def block(n, later, acc, masked):
            st_scores(0, n)
            later = st_softplus(0, later, n if masked else None)
            st_suffix(0)
            st_weights(0, n if masked else None)
            return later, st_values(0, n, acc)

        carry = block(1, *block(0, zero_later, zero_acc, True), True)
        _, acc = lax.fori_loop(n_sub, nblk, lambda n, c: block(n, *c, False), carry)
        acc_ref[...] = acc

    @pl.when(i >= 2)
    def _():
        def trip(n, p, later, acc, first=-1, last=4):
            static = isinstance(n, int)
            if first < 0 <= last:
                acc = st_values(p, n, acc)
            if first < 1 <= last:
                st_weights(1 - p, n + 1 if static and n + 1 < n_sub else None)
            if first < 2 <= last:
                st_suffix(p)
            if first < 3 <= last:
                later = st_softplus(1 - p, later, n + 3 if static and n + 3 < n_sub else None)
            if first < 4 <= last:
                st_scores(p, n + 4)
            return later, acc

        c = (zero_later, zero_acc)
        for n in range(-4, 2):
            c = trip(n, n % 2, *c, first=-n - 1)

        def two_trips(pair, c):
            n = 2 * pair + 2
            return trip(n + 1, 1, *trip(n, 0, *c))

        c = lax.fori_loop(0, i - 2, two_trips, c)
        for d in range(4):
            c = trip(nblk - 4 + d, d % 2, *c, last=3 - d)
        acc_ref[...] = c[1]

    o_ref[...] = acc_ref[...].astype(o_ref.dtype)


def sb_prompt(proj, k_buf, v_buf, bias, w_up, w_down, *, layer, batch, seq, heads, head_dim, tq, tk, pieces):
    assert head_dim == LANES
    nq = seq // tq
    steps = batch * heads * nq
    (_, du, fu), (_, fd, dd) = w_up.shape, w_down.shape
    assert du % steps == 0 and fd % steps == 0
    step = lambda b, h, i: (b * heads + h) * nq + i
    kernel = functools.partial(_sb_prompt_kernel, tq=tq, tk=tk, scale=head_dim ** -0.5, pieces=pieces)
    kv_spec = pl.BlockSpec((None, seq, head_dim), lambda b, h, i: (layer, b, h))
    return pl.pallas_call(
        kernel,
        grid=(batch, heads, nq),
        in_specs=[
            pl.BlockSpec(memory_space=pltpu.SMEM),
            pl.BlockSpec((tq, head_dim), lambda b, h, i: (b * nq + i, h)),
            kv_spec, kv_spec,
            pl.BlockSpec((None, du // steps, fu), lambda b, h, i: (layer, step(b, h, i), 0)),
            pl.BlockSpec((None, fd // steps, dd), lambda b, h, i: (layer, step(b, h, i), 0)),
        ],
        out_specs=[
            pl.BlockSpec((tq, head_dim), lambda b, h, i: (b * nq + i, h)),
            pl.BlockSpec((du // steps, fu), lambda b, h, i: (step(b, h, i), 0)),
            pl.BlockSpec((fd // steps, dd), lambda b, h, i: (step(b, h, i), 0)),
        ],
        out_shape=[jax.ShapeDtypeStruct((batch * seq, heads * head_dim), BF16),
                   jax.ShapeDtypeStruct((du, fu), BF16), jax.ShapeDtypeStruct((fd, dd), BF16)],
        scratch_shapes=[
            pltpu.VMEM((seq, head_dim), BF16), pltpu.VMEM((seq, head_dim), BF16),
            pltpu.VMEM((2, tq, tk), F32), pltpu.VMEM((2, tq, tk), F32), pltpu.VMEM((2, tq, pieces * tk), BF16),
            pltpu.VMEM((2, tq, tk), F32), pltpu.VMEM((2, tq, tk), BF16), pltpu.VMEM((tq, head_dim), F32),
        ],
        compiler_params=_params("parallel", "parallel", "arbitrary"),
        name="sb_prompt",
    )(bias, proj, k_buf, v_buf, w_up, w_down)


def _hgrn_gates(qb, fpre, lb, key_dim):
    e = jnp.exp(-jnp.abs(fpre))
    r = 1.0 / (1.0 + e)
    log_sig = jnp.minimum(fpre, 0.0) + jnp.log(r)
    sig_neg = jnp.where(fpre >= 0.0, e * r, r)
    a = jnp.log(lb)
    b = jnp.log(1.0 - lb) + log_sig
    logf = jnp.maximum(a, b) + jnp.log(1.0 + jnp.exp(-jnp.abs(a - b)))
    kh = (1.0 - lb) * sig_neg
    qh = qb * _sigmoid(qb) * (key_dim ** -0.5)
    return qh, kh, logf


def _hgrn_out(o, norm_w, gate):
    return _rms_scale(o, norm_w) * (gate * _sigmoid(gate))


def _level_mask_table(chunk):
    t = np.arange(chunk)[:, None]
    s = np.arange(chunk)[None, :]
    table = np.full((chunk, chunk), -1, np.int32)
    m, lvl = SUBLANES, 0
    while m < chunk:
        sibling = (t // (2 * m) == s // (2 * m)) & (t % (2 * m) >= m) & (s % (2 * m) < m)
        table[sibling] = lvl
        m, lvl = 2 * m, lvl + 1
    return table


def _hgrn_prompt_kernel(q_ref, f_ref, i_ref, g_ref, lb_ref, nw_ref, lm_ref, o_ref, s_ref, state_ref, *, chunk, hps):
    hb = pl.program_id(1)
    ci = pl.program_id(2)
    kd = q_ref.shape[-1] // hps

    @pl.when(ci == 0)
    def _():
        state_ref[...] = jnp.zeros_like(state_ref)

    row = lax.broadcasted_iota(jnp.int32, (chunk, chunk), 0)
    col = lax.broadcasted_iota(jnp.int32, (chunk, chunk), 1)
    prefix_incl = jnp.where(col <= row, 1.0, 0.0).astype(BF16)
    rows = lax.broadcasted_iota(jnp.int32, (chunk, kd), 0)
    lm = lm_ref[...]
    groups = chunk // SUBLANES
    sub = lax.broadcasted_iota(jnp.int32, (groups, SUBLANES, kd), 1)

    for hh in range(hps):
        lanes = slice(hh * kd, (hh + 1) * kd)
        lb = lb_ref[pl.ds(hb * hps + hh, 1), :]
        qh, kh, logf = _hgrn_gates(q_ref[:, lanes], f_ref[:, lanes], lb, kd)
        vh = i_ref[:, lanes]
        vh_b = vh.astype(BF16)

        b = sum(_dot(prefix_incl, part) for part in _split_bf16(logf, 3))

        state = state_ref[hh]
        o = _dot((qh * jnp.exp(b)).astype(BF16), state.astype(BF16))

        scores = jnp.zeros((chunk, chunk), F32)
        m, lvl = SUBLANES, 0
        while m < chunk:
            bnd = jnp.concatenate(
                [jnp.broadcast_to(b[p + m - 1:p + m, :], (2 * m, kd)) for p in range(0, chunk, 2 * m)], axis=0)
            second = (rows & (2 * m - 1)) >= m
            d = b - bnd
            qt = jnp.where(second, qh * jnp.exp(d), 0.0)
            kt = jnp.where(second, 0.0, kh * jnp.exp(-d))
            scores = jnp.where(lm == lvl, _dot_nt(qt.astype(BF16), kt.astype(BF16)), scores)
            m, lvl = 2 * m, lvl + 1
        o = o + _dot(scores.astype(BF16), vh_b)

        q3 = qh.reshape(groups, SUBLANES, kd)
        k3 = kh.reshape(groups, SUBLANES, kd)
        b3 = b.reshape(groups, SUBLANES, kd)
        v3 = vh.reshape(groups, SUBLANES, kd)
        o3 = jnp.zeros((groups, SUBLANES, kd), F32)
        for s in range(SUBLANES):
            decay = jnp.exp(b3 - b3[:, s:s + 1, :])
            a = jnp.sum(q3 * k3[:, s:s + 1, :] * decay, axis=-1, keepdims=True)
            o3 = o3 + jnp.where(sub >= s, a, 0.0) * v3[:, s:s + 1, :]
        o = o + o3.reshape(chunk, kd)

        b_last = b[chunk - 1:chunk, :]
        k_dec = kh * jnp.exp(b_last - b)
        decay_col = jnp.broadcast_to(jnp.exp(b_last), (kd, kd)).T
        state_ref[hh] = decay_col * state + _dot(k_dec.T.astype(BF16), vh_b)

        o_ref[:, lanes] = _hgrn_out(o, nw_ref[pl.ds(hb * hps + hh, 1), :], g_ref[:, lanes]).astype(o_ref.dtype)

    @pl.when(ci == pl.num_programs(2) - 1)
    def _():
        s_ref[0] = state_ref[...]


def _merge_out_kernel(*refs, gate_blocks):
    n_in = 3 + 2 * gate_blocks
    prompt, sample = refs[:n_in], refs[n_in:2 * n_in]
    wa_ref, wb_ref, wo_ref, g_ref, o_ref, os_ref = refs[2 * n_in:]

    rows = lambda k, dtype=F32: jnp.concatenate([prompt[k][...].astype(dtype), sample[k][...].astype(dtype)], axis=0)
    gate = lambda ks: jnp.concatenate([_sigmoid(rows(k)) for k in ks], axis=1)
    a = _dot(rows(0, BF16), wa_ref[...])
    b = _dot(rows(1, BF16), wb_ref[...])
    merged = gate(range(2, 2 + gate_blocks)) * a + gate(range(2 + gate_blocks, 2 + 2 * gate_blocks)) * b
    y = _dot(merged.astype(BF16), wo_ref[...])
    out = rows(n_in - 1) + _rms_scale(y, g_ref[...])
    tm = o_ref.shape[0]
    o_ref[...] = out[:tm]
    os_ref[...] = out[tm:]


def merge_out(prompt, sample, wa_all, wb_all, wo_all, gain, *, layer, tm, gate_off):
    m, d = prompt[3].shape
    rows_s = sample[3].shape[0]
    wa_rows, wb_rows = wa_all.shape[1], wb_all.shape[1]
    gw = int(np.gcd(gate_off, d))
    gate_blocks = d // gw
    const = lambda *shape: pl.BlockSpec((None, *shape), lambda i: (layer, 0, 0), pipeline_mode=pl.Buffered(1))

    def group(rows, row_index):
        at = lambda c: (lambda i: (row_index(i), c))
        return ([pl.BlockSpec((rows, wa_rows), at(0)), pl.BlockSpec((rows, wb_rows), at(0))]
                + [pl.BlockSpec((rows, gw), at(gate_off // gw + c)) for c in range(2 * gate_blocks)]
                + [pl.BlockSpec((rows, d), at(0))])

    operands = lambda t: (t[0], t[1], *([t[2]] * (2 * gate_blocks)), t[3])
    return pl.pallas_call(
        functools.partial(_merge_out_kernel, gate_blocks=gate_blocks),
        grid=(m // tm,),
        in_specs=group(tm, lambda i: i) + group(rows_s, lambda i: 0)
        + [const(wa_rows, d), const(wb_rows, d), const(d, d), pl.BlockSpec((1, d), lambda i: (0, 0))],
        out_specs=[pl.BlockSpec((tm, d), lambda i: (i, 0)), pl.BlockSpec((rows_s, d), lambda i: (0, 0))],
        out_shape=[jax.ShapeDtypeStruct((m, d), F32), jax.ShapeDtypeStruct((rows_s, d), F32)],
        compiler_params=_params("arbitrary"),
        name="merge_out_proj",
    )(*operands(prompt), *operands(sample), wa_all, wb_all, wo_all, gain.reshape(1, d))


def _mlp_kernel(x_ref, xs_ref, gpre_ref, wu_ref, wd_ref, gpost_ref, *rest, feeds_next):
    if feeds_next:
        gnext_ref, o_ref, os_ref, xn_ref, xns_ref, hn_ref, acc_ref = rest
    else:
        o_ref, os_ref, hn_ref, acc_ref = rest
    j = pl.program_id(1)
    tm = x_ref.shape[0]

    @pl.when(j == 0)
    def _():
        hn_ref[:tm, :] = _rms_scale(x_ref[...], gpre_ref[...]).astype(BF16)
        hn_ref[tm:, :] = _rms_scale(xs_ref[...], gpre_ref[...]).astype(BF16)
        acc_ref[...] = jnp.zeros_like(acc_ref)

    u = jnp.maximum(_dot(hn_ref[...], wu_ref[...]), 0.0)
    acc_ref[...] += _dot((u * u).astype(BF16), wd_ref[...])

    @pl.when(j == pl.num_programs(1) - 1)
    def _():
        out = x_ref[...] + _rms_scale(acc_ref[:tm, :], gpost_ref[...])
        outs = xs_ref[...] + _rms_scale(acc_ref[tm:, :], gpost_ref[...])
        o_ref[...] = out
        os_ref[...] = outs
        if feeds_next:
            xn_ref[...] = _rms_scale(out, gnext_ref[...]).astype(BF16)
            xns_ref[...] = _rms_scale(outs, gnext_ref[...]).astype(BF16)


def mlp(x, xs, gpre, wu, wd, gpost, gnext, *, tm, tf):
    m, d = x.shape
    rows_s = xs.shape[0]
    f = wu.shape[1]
    vec = pl.BlockSpec((1, d), lambda i, j: (0, 0))
    rows = pl.BlockSpec((tm, d), lambda i, j: (i, 0))
    sample = pl.BlockSpec((rows_s, d), lambda i, j: (0, 0))
    feeds_next = gnext is not None
    out_specs = [rows, sample] + ([rows, sample] if feeds_next else [])
    out_shape = [jax.ShapeDtypeStruct((m, d), F32), jax.ShapeDtypeStruct((rows_s, d), F32)]
    if feeds_next:
        out_shape += [jax.ShapeDtypeStruct((m, d), BF16), jax.ShapeDtypeStruct((rows_s, d), BF16)]
    return pl.pallas_call(
        functools.partial(_mlp_kernel, feeds_next=feeds_next),
        grid=(m // tm, f // tf),
        in_specs=[
            rows, sample, vec,
            pl.BlockSpec((d, tf), lambda i, j: (0, j)),
            pl.BlockSpec((tf, d), lambda i, j: (j, 0)),
            vec,
        ] + ([vec] if feeds_next else []),
        out_specs=out_specs,
        out_shape=out_shape,
        scratch_shapes=[pltpu.VMEM((tm + rows_s, d), BF16), pltpu.VMEM((tm + rows_s, d), F32)],
        compiler_params=_params("arbitrary", "arbitrary"),
        name="mlp",
    )(x, xs, gpre.reshape(1, d), wu, wd, gpost.reshape(1, d), *([gnext.reshape(1, d)] if feeds_next else []))


def _sb_sample_kernel(pt_ref, q_ref, bias_ref, kn_ref, vn_ref, *rest, pages_per_step, heads, page, past, scale,
                      step_axis):
    del pt_ref
    k_refs = rest[:pages_per_step]
    v_refs = rest[pages_per_step:2 * pages_per_step]
    o_ref, later_ref, acc_ref = rest[2 * pages_per_step:]
    g = pl.program_id(step_axis)
    hp, hd = q_ref.shape[1], q_ref.shape[2]
    width = page * heads
    nblk = width // LANES

    qs = q_ref[0] * scale
    q = qs.astype(BF16)

    @pl.when(g == 0)
    def _():
        z = jnp.sum(qs * kn_ref[0], axis=-1, keepdims=True) + bias_ref[:, 0:1]
        k_pos = past + lax.broadcasted_iota(jnp.int32, (hp, 1), 1)
        mask = k_pos < past
        sp = _softplus(z)
        later_ref[...] = jnp.broadcast_to(jnp.where(mask, -sp, 0.0), (hp, LANES))
        acc_ref[...] = jnp.where(mask, jnp.exp(z - sp), 0.0) * vn_ref[0]

    head_row = lax.broadcasted_iota(jnp.int32, (hp, width), 0)
    head_lane = lax.broadcasted_iota(jnp.int32, (hp, width), 1) % heads
    own = head_row == head_lane
    r = lax.broadcasted_iota(jnp.int32, (LANES, 2 * LANES), 0)
    c = lax.broadcasted_iota(jnp.int32, (LANES, 2 * LANES), 1)
    neg_sum = jnp.where((c >= LANES) | ((r % heads == c % heads) & (r // heads >= c // heads)), -1.0, 0.0).astype(BF16)
    bias = bias_ref[...]

    order = list(reversed(range(pages_per_step)))
    zs = [_dot_nt(q, k_refs[i][0, 0].reshape(width, hd).astype(BF16)) + bias for i in order]
    sps = [jnp.where(own, _softplus(z), 0.0) for z in zs]
    stacked = jnp.concatenate(
        [sp[:, b * LANES:(b + 1) * LANES] for sp in sps for b in reversed(range(nblk))], axis=0)
    s_hi, s_lo = _split_bf16(stacked, 2)
    sums = _dot(s_hi, neg_sum) + _dot(s_lo, neg_sum)

    later = later_ref[...]
    acc = acc_ref[...]
    for n, i in enumerate(order):
        pieces = [None] * nblk
        for m, b in enumerate(reversed(range(nblk))):
            rows = slice((n * nblk + m) * hp, (n * nblk + m + 1) * hp)
            blk = slice(b * LANES, (b + 1) * LANES)
            pieces[b] = jnp.where(own[:, blk], jnp.exp(zs[n][:, blk] + sums[rows, :LANES] + later), 0.0)
            later = later + sums[rows, LANES:]
        w = jnp.concatenate(pieces, axis=1).astype(BF16)
        acc = acc + _dot(w, v_refs[i][0, 0].reshape(width, hd).astype(BF16))
    later_ref[...] = later
    acc_ref[...] = acc

    @pl.when(g == pl.num_programs(step_axis) - 1)
    def _():
        o_ref[0] = acc


def _hgrn_prompt_sb_sample_kernel(pt_ref, *refs, hgrn, sample):
    n_hg, n_sm = 7, 4 + 2 * sample["pages_per_step"]
    hg_in, sm_in = refs[:n_hg], refs[n_hg:n_hg + n_sm]
    o_hg, s_hg, o_sm, state_ref, later_ref, acc_ref = refs[n_hg + n_sm:]
    _hgrn_prompt_kernel(*hg_in, o_hg, s_hg, state_ref, **hgrn)
    _sb_sample_kernel(pt_ref, *sm_in, o_sm, later_ref, acc_ref, **sample, step_axis=2)


def hgrn_prompt_sb_sample(proj, lb, norm_w, q, bias_b, k_new, v_new, cache_k, cache_v, page_table, *, layer, batch, seq,
                          heads, key_dim, col0, chunk, hps, pages_per_step):
    nb, hp, hd = q.shape
    _, _, page, cache_heads, _ = cache_k.shape
    n_pages = page_table.shape[1]
    nc = seq // chunk
    head_blocks = heads // hps
    assert key_dim == LANES and hd == LANES and LANES % cache_heads == 0 and col0 % hps == 0
    assert nb == batch * head_blocks and n_pages == nc * pages_per_step
    lm = jnp.asarray(_level_mask_table(chunk))

    def seg(k):
        return pl.BlockSpec((chunk, hps * key_dim), lambda b, h, c, pt: (b * nc + c, (col0 + k * heads) // hps + h))

    whole = lambda shape: pl.BlockSpec(shape, lambda b, h, c, pt: (0,) * len(shape))

    def page_spec(i):
        def index(b, h, c, pt):
            return (layer, pt[b * head_blocks + h, (nc - 1 - c) * pages_per_step + i], 0, 0, 0)
        return pl.BlockSpec((1, 1, page, cache_heads, hd), index)

    per_sample = pl.BlockSpec((1, hp, hd), lambda b, h, c, pt: (b * head_blocks + h, 0, 0))
    grid_spec = pltpu.PrefetchScalarGridSpec(
        num_scalar_prefetch=1,
        grid=(batch, head_blocks, nc),
        in_specs=[seg(0), seg(1), seg(2), seg(3), whole((heads, key_dim)), whole((heads, key_dim)), whole((chunk, chunk)),
                  per_sample, whole((hp, page * cache_heads)), per_sample, per_sample]
        + [page_spec(i) for i in range(pages_per_step)] * 2,
        out_specs=[
            pl.BlockSpec((chunk, hps * key_dim), lambda b, h, c, pt: (b * nc + c, h)),
            pl.BlockSpec((1, hps, key_dim, key_dim), lambda b, h, c, pt: (b, h, 0, 0)),
            per_sample,
        ],
        scratch_shapes=[pltpu.VMEM((hps, key_dim, key_dim), F32), pltpu.VMEM((hp, LANES), F32),
                        pltpu.VMEM((hp, hd), F32)],
    )
    kernel = functools.partial(
        _hgrn_prompt_sb_sample_kernel,
        hgrn=dict(chunk=chunk, hps=hps),
        sample=dict(pages_per_step=pages_per_step, heads=cache_heads, page=page, past=n_pages * page, scale=hd ** -0.5))
    return pl.pallas_call(
        kernel,
        grid_spec=grid_spec,
        out_shape=[
            jax.ShapeDtypeStruct((batch * seq, heads * key_dim), BF16),
            jax.ShapeDtypeStruct((batch, heads, key_dim, key_dim), F32),
            jax.ShapeDtypeStruct((nb, hp, hd), F32),
        ],
        compiler_params=_params("parallel", "parallel", "arbitrary"),
        name="hgrn_prompt_sb_sample",
    )(page_table, proj, proj, proj, proj, lb, norm_w, lm, q, bias_b, k_new, v_new,
      *([cache_k] * pages_per_step), *([cache_v] * pages_per_step))


def _hgrn_sample_kernel(q_ref, f_ref, i_ref, g_ref, lb_ref, nw_ref, s_ref, o_ref, so_ref, *, samples):
    h = pl.program_id(0)
    kd = q_ref.shape[-1]
    lb = lb_ref[pl.ds(h, 1), :]
    qh, kh, logf = _hgrn_gates(q_ref[...], f_ref[...], lb, kd)
    vh = i_ref[...]
    f = jnp.exp(logf)
    qk = jnp.sum(qh * kh, axis=-1, keepdims=True)
    o_ref[...] = jnp.zeros_like(o_ref)
    for b in range(samples):
        row = slice(b, b + 1)
        state = s_ref[b, 0]
        f_col = jnp.broadcast_to(f[row], (kd, kd)).T
        k_col = jnp.broadcast_to(kh[row], (kd, kd)).T
        so_ref[b, 0] = f_col * state + k_col * vh[row]
        q_dec = jnp.broadcast_to(qh[row] * f[row], (SUBLANES, kd)).astype(BF16)
        o = _dot(q_dec, state.astype(BF16))[0:1] + qk[row] * vh[row]
        o_ref[row, :] = _hgrn_out(o, nw_ref[pl.ds(h, 1), :], g_ref[row, :])


def hgrn_sample(proj, lb, norm_w, state, *, layer, heads, key_dim, col0, samples):
    rows = proj.shape[0]

    def seg(k):
        return pl.BlockSpec((rows, key_dim), lambda h: (0, col0 + k * heads + h))

    whole = pl.BlockSpec((heads, key_dim), lambda h: (0, 0))
    return pl.pallas_call(
        functools.partial(_hgrn_sample_kernel, samples=samples),
        grid=(heads,),
        in_specs=[seg(0), seg(1), seg(2), seg(3), whole, whole,
                  pl.BlockSpec((None, samples, 1, key_dim, key_dim), lambda h: (layer, 0, h, 0, 0))],
        out_specs=[
            pl.BlockSpec((rows, key_dim), lambda h: (0, h)),
            pl.BlockSpec((samples, 1, key_dim, key_dim), lambda h: (0, h, 0, 0)),
        ],
        out_shape=[
            jax.ShapeDtypeStruct((rows, heads * key_dim), F32),
            jax.ShapeDtypeStruct((samples, heads, key_dim, key_dim), F32),
        ],
        compiler_params=_params("parallel"),
        name="hgrn_sample",
    )(proj, proj, proj, proj, lb, norm_w, state)


SAMPLE_ROWS = 16
TILES = dict(norm_tm=1024, in_tm=2048, in_tn=512, sb_tq=512, sb_tk=256, sb_pieces=1, hg_chunk=256, hg_heads=2,
             merge_tm=256, mlp_tm=512, mlp_tf=1024, sample_pages=8)


def kernel(x_prompt, x_sample, cache_k, cache_v, state_hgrn, page_table, norm_mix_pre, norm_mix_post,
           norm_mlp_pre, norm_mlp_post, w_in, sb_bias, lower_bounds, hgrn_norm, w_branch_a, w_branch_b, w_out,
           w_up, w_down):
    batch, seq, d_model = x_prompt.shape
    nb, dec_seq, _ = x_sample.shape
    depth, _, page, heads, head_dim = cache_k.shape
    assert dec_seq == 1
    sb_width = heads * head_dim
    key_width = lower_bounds.shape[1]
    key_dim = key_width // heads
    hg_col0 = 3 * sb_width // key_dim
    gate_off = 3 * sb_width + 4 * key_width
    t = TILES

    lb_all = jnp.cumsum(jax.nn.softmax(lower_bounds.astype(F32), axis=0), axis=0)
    lb_all = jnp.maximum(lb_all - lb_all[:1], 0.0).reshape(depth, heads, key_dim)
    norm_w = hgrn_norm.reshape(depth, heads, key_dim)
    w_a_b, w_b_b, w_o_b = (w.astype(BF16) for w in (w_branch_a, w_branch_b, w_out))

    xp = x_prompt.reshape(batch * seq, d_model)
    xs = jnp.pad(x_sample.reshape(nb, d_model), ((0, SAMPLE_ROWS - nb), (0, 0)))
    pad_heads = lambda a: jnp.pad(a.reshape(nb, heads, head_dim), ((0, 0), (0, SAMPLE_ROWS - heads), (0, 0)))
    s_prompt, k_sample, v_sample, s_sample = [], [], [], []
    k_buf = v_buf = None
    xn, xns = norm_cast(xp, xs, norm_mix_pre[0], tm=t["norm_tm"])
    for l in range(depth):
        proj, proj_s, k_buf, v_buf = in_proj(xn, xns, w_in, layer=l, tm=t["in_tm"], tn=t["in_tn"], col0=sb_width,
                                             width=sb_width, k_buf=k_buf, v_buf=v_buf)
        o_a, w_up_b, w_down_b = sb_prompt(proj, k_buf, v_buf, sb_bias[l], w_up, w_down, layer=l, batch=batch, seq=seq,
                                          heads=heads, head_dim=head_dim, tq=t["sb_tq"], tk=t["sb_tk"],
                                          pieces=t["sb_pieces"])
        k_s = proj_s[:nb, sb_width:2 * sb_width]
        v_s = proj_s[:nb, 2 * sb_width:3 * sb_width]
        bias_b = jnp.pad(jnp.broadcast_to(sb_bias[l][:, None], (heads, page * heads)),
                         ((0, SAMPLE_ROWS - heads), (0, 0)))
        o_b, s_p, o_as = hgrn_prompt_sb_sample(
            proj, lb_all[l], norm_w[l], pad_heads(proj_s[:nb, :sb_width]), bias_b, pad_heads(k_s), pad_heads(v_s),
            cache_k, cache_v, page_table, layer=l, batch=batch, seq=seq, heads=heads, key_dim=key_dim, col0=hg_col0,
            chunk=t["hg_chunk"], hps=t["hg_heads"], pages_per_step=t["sample_pages"])
        o_as = jnp.pad(o_as[:, :heads].reshape(nb, sb_width), ((0, SAMPLE_ROWS - nb), (0, 0)))
        o_bs, s_s = hgrn_sample(proj_s, lb_all[l], norm_w[l], state_hgrn, layer=l, heads=heads, key_dim=key_dim,
                                col0=hg_col0, samples=nb)
        xp, xs = merge_out((o_a, o_b, proj, xp), (o_as, o_bs, proj_s, xs), w_a_b, w_b_b, w_o_b, norm_mix_post[l],
                           layer=l, tm=t["merge_tm"], gate_off=gate_off)
        xp, xs, *normed = mlp(xp, xs, norm_mlp_pre[l], w_up_b, w_down_b, norm_mlp_post[l],
                              norm_mix_pre[l + 1] if l + 1 < depth else None, tm=t["mlp_tm"], tf=t["mlp_tf"])
        if normed:
            xn, xns = normed
        s_prompt.append(s_p)
        k_sample.append(k_s.reshape(nb, dec_seq, heads, head_dim))
        v_sample.append(v_s.reshape(nb, dec_seq, heads, head_dim))
        s_sample.append(s_s)

    k_p, v_p = (buf.reshape(depth, batch, seq, heads, head_dim) for buf in (k_buf, v_buf))
    return (xp.reshape(batch, seq, d_model), xs[:nb].reshape(nb, dec_seq, d_model), k_p, v_p, jnp.stack(s_prompt),
            jnp.stack(k_sample), jnp.stack(v_sample), jnp.stack(s_sample))
```
